```python
import math
import jax, jax.numpy as jnp
from jax import lax
import numpy as np

D_MODEL = 1024
BATCH = 4
SEQ = 8192
DEPTH = 1

CHUNK = 64
NORM_EPS = 1e-5
SB_HEADS = 8
SB_HEAD_DIM = 64
SB_WIDTH = SB_HEADS * SB_HEAD_DIM
Q_BLOCK = 128
POOL_WINDOWS = (2, 4, 8, 16)
POOL_GROUPS = len(POOL_WINDOWS)
POOL_GROUP_DIM = 128
POOL_WIDTH = POOL_GROUPS * POOL_GROUP_DIM
MIX_WIDTH = SB_WIDTH + POOL_WIDTH
IN_PROJ_WIDTH = 3 * SB_WIDTH + POOL_WIDTH
N_EXPERTS = 32
TOP_K = 4
D_FF = D_MODEL
SWIGLU_LIMIT = 7.0
SWIGLU_ALPHA = 1.702
EXPERT_BLOCK = 256

kernel_name = "stickbreak_pool_hybrid_moe"


def rms_norm(x, g):
    xf = x.astype(jnp.float32)
    y = xf * lax.rsqrt(jnp.mean(xf * xf, axis=-1, keepdims=True) + NORM_EPS)
    return (y * g.astype(jnp.float32)).astype(x.dtype)


def stick_breaking_attention(q, k, v):
    B, S, H, dh = q.shape
    scale = 1.0 / math.sqrt(dh)
    qh = q.transpose(0, 2, 1, 3).astype(jnp.float32)
    kh = k.transpose(0, 2, 1, 3).astype(jnp.float32)
    vh = v.transpose(0, 2, 1, 3).astype(jnp.float32)
    n_blocks = S // Q_BLOCK
    q_blocks = qh.reshape(B, H, n_blocks, Q_BLOCK, dh).transpose(2, 0, 1, 3, 4)
    starts = jnp.arange(n_blocks, dtype=jnp.int32) * Q_BLOCK
    key_pos = jnp.arange(S, dtype=jnp.int32)[None, :]

    def one_block(args):
        q_blk, t0 = args
        z = jnp.einsum('bhqd,bhkd->bhqk', q_blk, kh) * scale
        qry_pos = t0 + jnp.arange(Q_BLOCK, dtype=jnp.int32)[:, None]
        past = key_pos < qry_pos
        log_keep = jnp.where(past, jax.nn.log_sigmoid(-z), 0.0)
        after = lax.cumsum(log_keep, axis=3, reverse=True) - log_keep
        a = jnp.where(past, jnp.exp(jax.nn.log_sigmoid(z) + after), 0.0)
        return jnp.einsum('bhqk,bhkd->bhqd', a, vh)

    out = lax.map(one_block, (q_blocks, starts))
    out = out.transpose(1, 0, 3, 2, 4).reshape(B, S, H, dh)
    return out.astype(q.dtype)


def multiscale_pool(u, w_pool, pool_scale):
    B, S, _ = u.shape
    p = u.reshape(B, S, POOL_GROUPS, POOL_GROUP_DIM).astype(jnp.float32)
    c0 = jnp.concatenate([jnp.zeros((B, 1, POOL_GROUPS, POOL_GROUP_DIM), jnp.float32),
                          jnp.cumsum(p, axis=1)], axis=1)
    win = jnp.array(POOL_WINDOWS, dtype=jnp.int32)
    t = jnp.arange(S, dtype=jnp.int32)[:, None]
    lo = jnp.maximum(t + 1 - win[None, :], 0)
    cnt = (t + 1 - lo).astype(jnp.float32)
    lower = c0[:, lo, jnp.arange(POOL_GROUPS)[None, :], :]
    mix = (c0[:, 1:] - lower) / cnt[None, :, :, None] - p
    y = jnp.einsum('bsgc,gcd->bsgd', mix, w_pool.astype(jnp.float32))
    y = y * pool_scale.astype(jnp.float32)
    return y.reshape(B, S, POOL_WIDTH).astype(u.dtype)


def expert_ffn(xb, e, w_gate_up, b_gate_up, w_down, b_down):
    gu = xb @ w_gate_up[e] + b_gate_up[e]
    gate = jnp.minimum(gu[:, :D_FF], SWIGLU_LIMIT)
    up = jnp.clip(gu[:, D_FF:], -SWIGLU_LIMIT, SWIGLU_LIMIT)
    h = (up + 1.0) * gate * jax.nn.sigmoid(SWIGLU_ALPHA * gate)
    return h @ w_down[e] + b_down[e]


def moe_ffn(h, w_router, b_router, w_gate_up, b_gate_up, w_down, b_down):
    B, S, D = h.shape
    n_tok = B * S
    xf = h.reshape(n_tok, D)
    logits = (xf @ w_router + b_router).astype(jnp.float32)
    top_val, top_idx = lax.top_k(logits, TOP_K)
    gates = jax.nn.softmax(top_val, axis=-1)
    n_assign = n_tok * TOP_K
    flat_e = top_idx.reshape(-1).astype(jnp.int32)
    flat_tok = jnp.arange(n_assign, dtype=jnp.int32) // TOP_K
    order = jnp.argsort(flat_e)
    se, stok, sgate = flat_e[order], flat_tok[order], gates.reshape(-1)[order]
    counts = jnp.bincount(flat_e, length=N_EXPERTS)
    padded = (counts + EXPERT_BLOCK - 1) // EXPERT_BLOCK * EXPERT_BLOCK
    start = jnp.cumsum(counts) - counts
    pend = jnp.cumsum(padded)
    pstart = pend - padded
    dest = pstart[se] + (jnp.arange(n_assign, dtype=jnp.int32) - start[se])
    n_blocks = -(-n_assign // EXPERT_BLOCK) + N_EXPERTS
    buf = jnp.zeros((n_blocks * EXPERT_BLOCK, D), xf.dtype).at[dest].set(xf[stok])
    block_e = jnp.minimum(
        jnp.searchsorted(pend, jnp.arange(n_blocks, dtype=jnp.int32) * EXPERT_BLOCK, side='right'),
        N_EXPERTS - 1).astype(jnp.int32)
    yb = lax.map(lambda a: expert_ffn(a[0], a[1], w_gate_up, b_gate_up, w_down, b_down),
                 (buf.reshape(n_blocks, EXPERT_BLOCK, D), block_e))
    y_rows = yb.reshape(-1, D)[dest].astype(jnp.float32) * sgate[:, None]
    out = jax.ops.segment_sum(y_rows, stok, num_segments=n_tok)
    return out.reshape(B, S, D).astype(h.dtype)


def setup_inputs(seed: int = 0) -> dict:
    key = jax.random.key(seed)
    ks = jax.random.split(key, 17)
    f32 = jnp.float32
    nrm = lambda k, shape, s: jax.random.normal(k, shape, f32) * s
    L = DEPTH
    return {
        "x": jax.random.normal(ks[0], (BATCH, SEQ, D_MODEL), f32),
        "norm1_g": 1.0 + nrm(ks[1], (L, D_MODEL), 0.02),
        "w_in": nrm(ks[2], (L, D_MODEL, IN_PROJ_WIDTH), D_MODEL ** -0.5),
        "sb_norm_g": 1.0 + nrm(ks[3], (L, SB_WIDTH), 0.02),
        "w_pool": nrm(ks[4], (L, POOL_GROUPS, POOL_GROUP_DIM, POOL_GROUP_DIM), POOL_GROUP_DIM ** -0.5),
        "pool_scale": 1.0 + nrm(ks[5], (L, POOL_GROUPS, POOL_GROUP_DIM), 0.02),
        "w_out": nrm(ks[6], (L, MIX_WIDTH, D_MODEL), MIX_WIDTH ** -0.5),
        "norm2_g": 1.0 + nrm(ks[7], (L, D_MODEL), 0.02),
        "w_router": nrm(ks[8], (L, D_MODEL, N_EXPERTS), D_MODEL ** -0.5),
        "b_router": nrm(ks[9], (L, N_EXPERTS), 0.01),
        "w_gate_up": nrm(ks[10], (L, N_EXPERTS, D_MODEL, 2 * D_FF), D_MODEL ** -0.5),
        "b_gate_up": nrm(ks[11], (L, N_EXPERTS, 2 * D_FF), 0.02),
        "w_down": nrm(ks[12], (L, N_EXPERTS, D_FF, D_MODEL), D_FF ** -0.5),
        "b_down": nrm(ks[13], (L, N_EXPERTS, D_MODEL), 0.02),
        "final_norm_g": 1.0 + nrm(ks[14], (D_MODEL,), 0.02),
    }


def reference(x, norm1_g, w_in, sb_norm_g, w_pool, pool_scale, w_out, norm2_g,
              w_router, b_router, w_gate_up, b_gate_up, w_down, b_down, final_norm_g):
    B, S, _ = x.shape
    for layer in range(DEPTH):
        h = rms_norm(x, norm1_g[layer])
        proj = h @ w_in[layer]
        q = proj[..., :SB_WIDTH].reshape(B, S, SB_HEADS, SB_HEAD_DIM)
        k = proj[..., SB_WIDTH:2 * SB_WIDTH].reshape(B, S, SB_HEADS, SB_HEAD_DIM)
        v = proj[..., 2 * SB_WIDTH:3 * SB_WIDTH].reshape(B, S, SB_HEADS, SB_HEAD_DIM)
        u = proj[..., 3 * SB_WIDTH:]
        sb = stick_breaking_attention(q, k, v)
        sb = rms_norm(sb, sb_norm_g[layer].reshape(SB_HEADS, SB_HEAD_DIM)).reshape(B, S, SB_WIDTH)
        pool = multiscale_pool(u, w_pool[layer], pool_scale[layer])
        x = x + jnp.concatenate([sb, pool], axis=-1) @ w_out[layer]
        h2 = rms_norm(x, norm2_g[layer])
        x = x + moe_ffn(h2, w_router[layer], b_router[layer], w_gate_up[layer],
                        b_gate_up[layer], w_down[layer], b_down[layer])
    return rms_norm(x, final_norm_g)
```

```python
import functools
import math

import jax
import jax.numpy as jnp
from jax import lax
from jax.experimental import pallas as pl
from jax.experimental.pallas import tpu as pltpu

NORM_EPS = 1e-5
SB_HEADS = 8
SB_HEAD_DIM = 64
SB_WIDTH = SB_HEADS * SB_HEAD_DIM
POOL_WINDOWS = (2, 4, 8, 16)
POOL_GROUP_DIM = 128
POOL_WIDTH = len(POOL_WINDOWS) * POOL_GROUP_DIM
POOL_HALO = 16
N_EXPERTS = 32
TOP_K = 4
SWIGLU_LIMIT = 7.0
SWIGLU_ALPHA = 1.702

LANES = 128
VMEM_LIMIT_BYTES = 56 * 1024 * 1024

PROJ_ROWS = 512
ATTN_BLOCK = 256
MIX_ROWS = 256
SCATTER_ROWS = 256
FFN_ROWS = 512
COMBINE_ROWS = 256

F32 = jnp.float32
BF16 = jnp.bfloat16


def _dot(a, b):
    return jnp.dot(a, b, preferred_element_type=F32)


def _dot_nt(a, b):
    return lax.dot_general(a, b, (((1,), (1,)), ((), ())), preferred_element_type=F32)


def _split_bf16(x):
    hi = x.astype(BF16)
    lo = (x - hi.astype(F32)).astype(BF16)
    return hi, lo


def _pack_bf16_pair(lo_f32, hi_f32):
    lo_bits = pltpu.bitcast(lo_f32.astype(BF16).astype(F32), jnp.uint32)
    hi_bits = pltpu.bitcast(hi_f32.astype(BF16).astype(F32), jnp.uint32)
    return (hi_bits & jnp.uint32(0xFFFF0000)) | (lo_bits >> 16)


def _unpack_bf16_pair(word):
    lo = pltpu.bitcast(word << 16, F32)
    hi = pltpu.bitcast(word & jnp.uint32(0xFFFF0000), F32)
    return lo, hi


def _in_proj_kernel(x_ref, g_ref, w_ref, q_ref, k_ref, v_ref, u_ref):
    x = x_ref[...]
    ms = jnp.mean(x * x, axis=-1, keepdims=True)
    h = (x * lax.rsqrt(ms + NORM_EPS) * g_ref[...]).astype(BF16)
    proj = _dot(h, w_ref[...])
    scale = 1.0 / math.sqrt(SB_HEAD_DIM)
    q_ref[...] = (proj[:, :SB_WIDTH] * scale).astype(BF16)
    k_ref[...] = proj[:, SB_WIDTH:2 * SB_WIDTH].astype(BF16)
    v_ref[...] = proj[:, 2 * SB_WIDTH:3 * SB_WIDTH].astype(BF16)
    u_ref[...] = proj[:, 3 * SB_WIDTH:].astype(BF16)


def _in_proj(xf, g, w_bf16):
    n, d = xf.shape
    tm = min(PROJ_ROWS, n)
    wide = w_bf16.shape[1]
    out = jax.ShapeDtypeStruct((n, SB_WIDTH), BF16)
    row_spec = pl.BlockSpec((tm, SB_WIDTH), lambda i: (i, 0))
    return pl.pallas_call(
        _in_proj_kernel,
        grid=(n // tm,),
        in_specs=[pl.BlockSpec((tm, d), lambda i: (i, 0)),
                  pl.BlockSpec((1, d), lambda i: (0, 0)),
                  pl.BlockSpec((d, wide), lambda i: (0, 0))],
        out_specs=[row_spec, row_spec, row_spec, row_spec],
        out_shape=[out, out, out, jax.ShapeDtypeStruct((n, POOL_WIDTH), BF16)],
        compiler_params=pltpu.CompilerParams(
            dimension_semantics=("parallel",), vmem_limit_bytes=VMEM_LIMIT_BYTES),
        name="in_proj",
    )(xf, g, w_bf16)


def _attention_kernel(q_ref, k_ref, v_ref, g_ref, o_ref, acc_ref):
    tq = q_ref.shape[0]
    tk = tq
    qi = pl.program_id(2)
    lane = lax.broadcasted_iota(jnp.int32, (tq, LANES), 1)
    q = q_ref[...]
    zero = jnp.zeros_like(q)
    q_heads = (jnp.where(lane < SB_HEAD_DIM, q, zero), jnp.where(lane >= SB_HEAD_DIM, q, zero))

    row = lax.broadcasted_iota(jnp.int32, (tk, tk), 0)
    col = lax.broadcasted_iota(jnp.int32, (tk, tk), 1)
    later = (row > col).astype(BF16)
    past = col < row

    def visit(k_blk, v_blk, carries, diagonal):
        new_carries = []
        for h in range(2):
            z = _dot_nt(q_heads[h], k_blk)
            softplus = jnp.maximum(z, 0.0) + jnp.log1p(jnp.exp(-jnp.abs(z)))
            log_keep = -softplus
            if diagonal:
                log_keep = jnp.where(past, log_keep, 0.0)
            hi, lo = _split_bf16(log_keep)
            after = _dot(hi, later) + _dot(lo, later) + carries[h]
            a = jnp.exp(z + log_keep + after)
            if diagonal:
                a = jnp.where(past, a, 0.0)
            acc_ref[h] += _dot(a.astype(BF16), v_blk)
            new_carries.append(carries[h] + jnp.sum(log_keep, axis=-1, keepdims=True))
        return tuple(new_carries)

    acc_ref[...] = jnp.zeros_like(acc_ref)
    start = pl.multiple_of(qi * tk, tk)
    zero_carry = jnp.zeros((tq, 1), F32)
    carries = visit(k_ref[pl.ds(start, tk), :], v_ref[pl.ds(start, tk), :],
                    (zero_carry, zero_carry), diagonal=True)

    def body(step, carries):
        off = pl.multiple_of((qi - 1 - step) * tk, tk)
        return visit(k_ref[pl.ds(off, tk), :], v_ref[pl.ds(off, tk), :], carries, diagonal=False)

    lax.fori_loop(0, qi, body, carries)

    o = jnp.where(lane < SB_HEAD_DIM, acc_ref[0], acc_ref[1])
    sq = o * o
    s0 = jnp.sum(jnp.where(lane < SB_HEAD_DIM, sq, 0.0), axis=-1, keepdims=True)
    s1 = jnp.sum(jnp.where(lane >= SB_HEAD_DIM, sq, 0.0), axis=-1, keepdims=True)
    ms = jnp.where(lane < SB_HEAD_DIM, s0, s1) * (1.0 / SB_HEAD_DIM)
    o_ref[...] = (o * lax.rsqrt(ms + NORM_EPS) * g_ref[...]).astype(o_ref.dtype)


def _attention(q, k, v, sb_g):
    b, s, _ = q.shape
    t = min(ATTN_BLOCK, s)
    pairs = SB_WIDTH // LANES
    blk = pl.BlockSpec((None, t, LANES), lambda bi, p, i: (bi, i, p))
    seq = pl.BlockSpec((None, s, LANES), lambda bi, p, i: (bi, 0, p))
    return pl.pallas_call(
        _attention_kernel,
        grid=(b, pairs, s // t),
        in_specs=[blk, seq, seq, pl.BlockSpec((1, LANES), lambda bi, p, i: (0, p))],
        out_specs=blk,
        out_shape=jax.ShapeDtypeStruct((b, s, SB_WIDTH), BF16),
        scratch_shapes=[pltpu.VMEM((2, t, LANES), F32)],
        compiler_params=pltpu.CompilerParams(
            dimension_semantics=("parallel", "parallel", "arbitrary"),
            vmem_limit_bytes=VMEM_LIMIT_BYTES),
        name="attention",
    )(q, k, v, sb_g)


def _mix_route_kernel(x_ref, sb_ref, u_ref, halo_ref, wout_ref, wpool_ref, pscale_ref, g2_ref,
                      wr_ref, br_ref,
                      x1_ref, h2p_ref, idx_ref, gate_ref, rank_ref, count_ref,
                      count_acc, *, tiles_per_seq):
    tm = x_ref.shape[0]
    i = pl.program_id(0)
    seq_tile = i % tiles_per_seq

    t_main = lax.broadcasted_iota(jnp.int32, (tm, tm), 0)
    s_main = lax.broadcasted_iota(jnp.int32, (tm, tm), 1)
    t_halo = lax.broadcasted_iota(jnp.int32, (tm, POOL_HALO), 0)
    s_halo = lax.broadcasted_iota(jnp.int32, (tm, POOL_HALO), 1) - POOL_HALO
    pos = seq_tile * tm + lax.broadcasted_iota(jnp.int32, (tm, 1), 0)
    u = u_ref[...]
    halo = jnp.where(seq_tile > 0, halo_ref[...], jnp.zeros_like(halo_ref))
    y = _dot(sb_ref[...], wout_ref[pl.ds(0, SB_WIDTH), :])
    for g, w in enumerate(POOL_WINDOWS):
        cols = pl.ds(g * POOL_GROUP_DIM, POOL_GROUP_DIM)
        band_main = ((s_main <= t_main) & (s_main > t_main - w)).astype(BF16)
        band_halo = (s_halo > t_halo - w).astype(BF16)
        p = u[:, g * POOL_GROUP_DIM:(g + 1) * POOL_GROUP_DIM]
        win = _dot(band_main, p) + _dot(band_halo, halo[:, g * POOL_GROUP_DIM:(g + 1) * POOL_GROUP_DIM])
        cnt = jnp.minimum(pos + 1, w).astype(F32)
        mix = win / cnt - p.astype(F32)
        pooled = _dot(mix.astype(BF16), wpool_ref[g]) * pscale_ref[:, cols]
        y += _dot(pooled.astype(BF16), wout_ref[pl.ds(SB_WIDTH + g * POOL_GROUP_DIM, POOL_GROUP_DIM), :])

    x1 = x_ref[...] + y
    x1_ref[...] = x1
    ms = jnp.mean(x1 * x1, axis=-1, keepdims=True)
    h2 = x1 * lax.rsqrt(ms + NORM_EPS) * g2_ref[...]
    half = h2.shape[1] // 2
    h2p_ref[...] = _pack_bf16_pair(h2[:, :half], h2[:, half:])

    h_hi, h_lo = _split_bf16(h2)
    w_hi = wr_ref[0]
    w_lo = wr_ref[1]
    logits = _dot_nt(w_hi, h_hi) + _dot_nt(w_hi, h_lo) + _dot_nt(w_lo, h_hi) + br_ref[...]

    eidx = lax.broadcasted_iota(jnp.int32, (N_EXPERTS, tm), 0)
    work = logits
    vals, ids, hots = [], [], []
    for _ in range(TOP_K):
        m = jnp.max(work, axis=0, keepdims=True)
        sel = jnp.min(jnp.where(work == m, eidx, N_EXPERTS), axis=0, keepdims=True)
        hot = eidx == sel
        work = jnp.where(hot, -jnp.inf, work)
        vals.append(m)
        ids.append(sel)
        hots.append(hot)
    exps = [jnp.exp(v - vals[0]) for v in vals]
    denom = exps[0] + exps[1] + exps[2] + exps[3]

    @pl.when(i == 0)
    def _():
        count_acc[...] = jnp.zeros_like(count_acc)

    chosen = hots[0] | hots[1] | hots[2] | hots[3]
    earlier = (t_main < s_main).astype(BF16)
    before = _dot(chosen.astype(BF16), earlier) + count_acc[:, 0:1]
    for r in range(TOP_K):
        idx_ref[r:r + 1, :] = ids[r]
        gate_ref[r:r + 1, :] = exps[r] / denom
        rank = jnp.sum(jnp.where(hots[r], before, 0.0), axis=0, keepdims=True)
        rank_ref[r:r + 1, :] = rank.astype(jnp.int32)
    count_acc[...] += jnp.sum(chosen.astype(F32), axis=1, keepdims=True)
    count_ref[...] = count_acc[...]


def _mix_route(xf, sb, u, w_out, w_pool, pool_scale, g2, wr_split, b_router, seq_len):
    n, d = xf.shape
    tm = min(MIX_ROWS, seq_len)
    tiles_per_seq = seq_len // tm
    halo_blocks = tm // POOL_HALO
    row = lambda width: pl.BlockSpec((tm, width), lambda i: (i, 0))
    full = lambda shape: pl.BlockSpec(shape, lambda i: (0,) * len(shape))
    tok = pl.BlockSpec((TOP_K, tm), lambda i: (0, i))
    return pl.pallas_call(
        functools.partial(_mix_route_kernel, tiles_per_seq=tiles_per_seq),
        grid=(n // tm,),
        in_specs=[row(d), row(SB_WIDTH), row(POOL_WIDTH),
                  pl.BlockSpec((POOL_HALO, POOL_WIDTH), lambda i: (jnp.maximum(i * halo_blocks - 1, 0), 0)),
                  full(w_out.shape), full(w_pool.shape), full(pool_scale.shape), full(g2.shape),
                  full(wr_split.shape), full(b_router.shape)],
        out_specs=[row(d), row(d // 2), tok, tok, tok, full((N_EXPERTS, LANES))],
        out_shape=[jax.ShapeDtypeStruct((n, d), F32),
                   jax.ShapeDtypeStruct((n, d // 2), jnp.uint32),
                   jax.ShapeDtypeStruct((TOP_K, n), jnp.int32),
                   jax.ShapeDtypeStruct((TOP_K, n), F32),
                   jax.ShapeDtypeStruct((TOP_K, n), jnp.int32),
                   jax.ShapeDtypeStruct((N_EXPERTS, LANES), F32)],
        scratch_shapes=[pltpu.VMEM((N_EXPERTS, LANES), F32)],
        compiler_params=pltpu.CompilerParams(
            dimension_semantics=("arbitrary",), vmem_limit_bytes=VMEM_LIMIT_BYTES),
        name="mix_route",
    )(xf, sb, u, u, w_out, w_pool, pool_scale, g2, wr_split, b_router)


def _scatter_rows_kernel(dest_ref, h_ref, buf_in_ref, buf_ref, sem):
    del buf_in_ref
    tm = h_ref.shape[0]

    def copy(t, r):
        return pltpu.make_async_copy(h_ref.at[pl.ds(t, 1), :],
                                     buf_ref.at[pl.ds(dest_ref[r, t], 1), :], sem)

    def start(t, c):
        for r in range(TOP_K):
            copy(t, r).start()
        return c

    def wait(t, c):
        for r in range(TOP_K):
            copy(t, r).wait()
        return c

    lax.fori_loop(0, tm, start, 0)
    lax.fori_loop(0, tm, wait, 0)


def _scatter_rows(dest, h2p, n_rows):
    n, width = h2p.shape
    tm = min(SCATTER_ROWS, n)
    zeros = jnp.zeros((n_rows, width), h2p.dtype)
    return pl.pallas_call(
        _scatter_rows_kernel,
        grid=(n // tm,),
        in_specs=[pl.BlockSpec((TOP_K, tm), lambda i: (0, i), memory_space=pltpu.SMEM),
                  pl.BlockSpec((tm, width), lambda i: (i, 0)),
                  pl.BlockSpec(memory_space=pl.ANY)],
        out_specs=pl.BlockSpec(memory_space=pl.ANY),
        out_shape=jax.ShapeDtypeStruct((n_rows, width), h2p.dtype),
        scratch_shapes=[pltpu.SemaphoreType.DMA(())],
        input_output_aliases={2: 0},
        compiler_params=pltpu.CompilerParams(
            dimension_semantics=("arbitrary",), vmem_limit_bytes=VMEM_LIMIT_BYTES),
        name="scatter_rows",
    )(dest, h2p, zeros)


def _expert_ffn_kernel(block_e_ref, n_used_ref, x_ref, wgu_ref, bgu_ref, wd_ref, bd_ref, y_ref,
                       wgu_bf, wd_bf):
    i = pl.program_id(0)
    d_ff = wd_ref.shape[0]
    half = x_ref.shape[1]

    @pl.when(i < n_used_ref[0])
    def _():
        prev = block_e_ref[jnp.maximum(i - 1, 0)]

        @pl.when((i == 0) | (block_e_ref[i] != prev))
        def _():
            wgu_bf[...] = wgu_ref[...].astype(BF16)
            wd_bf[...] = wd_ref[...].astype(BF16)

        x_lo, x_hi = _unpack_bf16_pair(x_ref[...])
        gu = (_dot(x_lo.astype(BF16), wgu_bf[pl.ds(0, half), :])
              + _dot(x_hi.astype(BF16), wgu_bf[pl.ds(half, half), :]) + bgu_ref[...])
        gate = jnp.minimum(gu[:, :d_ff], SWIGLU_LIMIT)
        up = jnp.clip(gu[:, d_ff:], -SWIGLU_LIMIT, SWIGLU_LIMIT)
        act = (up + 1.0) * gate * (1.0 / (1.0 + jnp.exp(-SWIGLU_ALPHA * gate)))
        y = _dot(act.astype(BF16), wd_bf[...]) + bd_ref[...]
        y_ref[...] = _pack_bf16_pair(y[:, :half], y[:, half:])

    @pl.when(i >= n_used_ref[0])
    def _():
        y_ref[...] = jnp.zeros_like(y_ref)


def _expert_ffn(block_e, n_used, buf, w_gate_up, b_gate_up, w_down, b_down):
    n_rows, half = buf.shape
    _, d, two_f = w_gate_up.shape
    d_ff = w_down.shape[1]
    n_blocks = n_rows // FFN_ROWS

    def rows(i, be, nu):
        return (jnp.minimum(i, nu[0] - 1), 0)

    def expert(i, be, nu):
        return (be[jnp.minimum(i, nu[0] - 1)], 0, 0)

    grid_spec = pltpu.PrefetchScalarGridSpec(
        num_scalar_prefetch=2,
        grid=(n_blocks,),
        in_specs=[pl.BlockSpec((FFN_ROWS, half), rows),
                  pl.BlockSpec((None, d, two_f), expert),
                  pl.BlockSpec((None, 1, two_f), expert),
                  pl.BlockSpec((None, d_ff, d), expert),
                  pl.BlockSpec((None, 1, d), expert)],
        out_specs=pl.BlockSpec((FFN_ROWS, half), lambda i, be, nu: (i, 0)),
        scratch_shapes=[pltpu.VMEM((d, two_f), BF16), pltpu.VMEM((d_ff, d), BF16)],
    )
    return pl.pallas_call(
        _expert_ffn_kernel,
        grid_spec=grid_spec,
        out_shape=jax.ShapeDtypeStruct((n_rows, half), jnp.uint32),
        compiler_params=pltpu.CompilerParams(
            dimension_semantics=("arbitrary",), vmem_limit_bytes=VMEM_LIMIT_BYTES),
        name="expert_ffn",
    )(block_e, n_used, buf, w_gate_up, b_gate_up, w_down, b_down)


def _combine_kernel(dest_ref, x1_ref, gate_ref, g_ref, y_ref, o_ref, rows_ref, sem):
    tm = x1_ref.shape[0]

    def copy(t, r):
        return pltpu.make_async_copy(y_ref.at[pl.ds(dest_ref[r, t], 1), :],
                                     rows_ref.at[r, pl.ds(t, 1), :], sem)

    def start(t, c):
        for r in range(TOP_K):
            copy(t, r).start()
        return c

    def wait(t, c):
        for r in range(TOP_K):
            copy(t, r).wait()
        return c

    lax.fori_loop(0, tm, start, 0)
    lax.fori_loop(0, tm, wait, 0)

    gates = gate_ref[...]
    x1 = x1_ref[...]
    half = rows_ref.shape[2]
    lo_sum = x1[:, :half]
    hi_sum = x1[:, half:]
    for r in range(TOP_K):
        lo, hi = _unpack_bf16_pair(rows_ref[r])
        g = gates[:, r:r + 1]
        lo_sum += g * lo
        hi_sum += g * hi
    ms = (jnp.sum(lo_sum * lo_sum, axis=-1, keepdims=True)
          + jnp.sum(hi_sum * hi_sum, axis=-1, keepdims=True)) * (1.0 / (2 * half))
    inv = lax.rsqrt(ms + NORM_EPS)
    o_ref[:, :half] = lo_sum * inv * g_ref[:, :half]
    o_ref[:, half:] = hi_sum * inv * g_ref[:, half:]


def _combine(dest, x1, gates_t, final_g, y):
    n, d = x1.shape
    tm = min(COMBINE_ROWS, n)
    return pl.pallas_call(
        _combine_kernel,
        grid=(n // tm,),
        in_specs=[pl.BlockSpec((TOP_K, tm), lambda i: (0, i), memory_space=pltpu.SMEM),
                  pl.BlockSpec((tm, d), lambda i: (i, 0)),
                  pl.BlockSpec((tm, TOP_K), lambda i: (i, 0)),
                  pl.BlockSpec((1, d), lambda i: (0, 0)),
                  pl.BlockSpec(memory_space=pl.ANY)],
        out_specs=pl.BlockSpec((tm, d), lambda i: (i, 0)),
        out_shape=jax.ShapeDtypeStruct((n, d), F32),
        scratch_shapes=[pltpu.VMEM((TOP_K, tm, d // 2), jnp.uint32), pltpu.SemaphoreType.DMA(())],
        compiler_params=pltpu.CompilerParams(
            dimension_semantics=("arbitrary",), vmem_limit_bytes=VMEM_LIMIT_BYTES),
        name="combine",
    )(dest, x1, gates_t, final_g, y)


def _layer(x, norm1_g, w_in, sb_norm_g, w_pool, pool_scale, w_out, norm2_g,
           w_router, b_router, w_gate_up, b_gate_up, w_down, b_down, out_g):
    b, s, d = x.shape
    n = b * s
    xf = x.reshape(n, d)

    q, k, v, u = _in_proj(xf, norm1_g.reshape(1, d), w_in.astype(BF16))
    sb = _attention(q.reshape(b, s, SB_WIDTH), k.reshape(b, s, SB_WIDTH), v.reshape(b, s, SB_WIDTH),
                    sb_norm_g.reshape(1, SB_WIDTH))

    wr_t = w_router.T
    wr_hi = wr_t.astype(BF16)
    wr_lo = (wr_t - wr_hi.astype(F32)).astype(BF16)
    x1, h2p, idx, gates, rank, counts = _mix_route(
        xf, sb.reshape(n, SB_WIDTH), u, w_out.astype(BF16), w_pool.astype(BF16),
        pool_scale.reshape(1, POOL_WIDTH), norm2_g.reshape(1, d),
        jnp.stack([wr_hi, wr_lo]), b_router.reshape(N_EXPERTS, 1), s)

    counts = counts[:, 0].astype(jnp.int32)
    padded = (counts + FFN_ROWS - 1) // FFN_ROWS * FFN_ROWS
    pend = jnp.cumsum(padded)
    pstart = pend - padded
    dest = pstart[idx] + rank
    n_blocks = -(-(n * TOP_K) // FFN_ROWS) + N_EXPERTS
    n_used = (pend[-1] // FFN_ROWS).astype(jnp.int32).reshape(1)
    block_e = jnp.minimum(
        jnp.searchsorted(pend, jnp.arange(n_blocks, dtype=jnp.int32) * FFN_ROWS, side="right"),
        N_EXPERTS - 1).astype(jnp.int32)

    buf = _scatter_rows(dest, h2p, n_blocks * FFN_ROWS)
    y = _expert_ffn(block_e, n_used, buf, w_gate_up, b_gate_up.reshape(N_EXPERTS, 1, -1),
                    w_down, b_down.reshape(N_EXPERTS, 1, -1))
    out = _combine(dest, x1, gates.T, out_g.reshape(1, d), y)
    return out.reshape(b, s, d)


def kernel(x, norm1_g, w_in, sb_norm_g, w_pool, pool_scale, w_out, norm2_g, w_router, b_router,
           w_gate_up, b_gate_up, w_down, b_down, final_norm_g):
    assert norm1_g.shape[0] == 1, "single-layer block"
    return _layer(x, norm1_g[0], w_in[0], sb_norm_g[0], w_pool[0], pool_scale[0], w_out[0],
                  norm2_g[0], w_router[0], b_router[0], w_gate_up[0], b_gate_up[0], w_down[0],
                  b_down[0], final_norm_g)
```

```python
import functools
import math

import jax
import jax.numpy as jnp
from jax import lax
from jax.experimental import pallas as pl
from jax.experimental.pallas import tpu as pltpu

NORM_EPS = 1e-5
SB_HEADS = 8
SB_HEAD_DIM = 64
SB_WIDTH = SB_HEADS * SB_HEAD_DIM
POOL_WINDOWS = (2, 4, 8, 16)
POOL_GROUP_DIM = 128
POOL_WIDTH = len(POOL_WINDOWS) * POOL_GROUP_DIM
POOL_HALO = 16
N_EXPERTS = 32
TOP_K = 4
SWIGLU_LIMIT = 7.0
SWIGLU_ALPHA = 1.702
EXP_UNDERFLOW_F32 = -105.0

LANES = 128
VMEM_LIMIT_BYTES = 56 * 1024 * 1024

PROJ_ROWS = 512
ATTN_BLOCK = 256
MIX_ROWS = 512
ROW_DMA_UNROLL = 4
SCATTER_ROWS = 256
FFN_ROWS = 512
COMBINE_ROWS = 256

F32 = jnp.float32
BF16 = jnp.bfloat16


def _dot(a, b):
    return jnp.dot(a, b, preferred_element_type=F32)


def _dot_nt(a, b):
    return lax.dot_general(a, b, (((1,), (1,)), ((), ())), preferred_element_type=F32)


def _split_bf16(x):
    hi = x.astype(BF16)
    lo = (x - hi.astype(F32)).astype(BF16)
    return hi, lo


def _pack_bf16_pair(lo_f32, hi_f32):
    lo_bits = pltpu.bitcast(lo_f32.astype(BF16).astype(F32), jnp.uint32)
    hi_bits = pltpu.bitcast(hi_f32.astype(BF16).astype(F32), jnp.uint32)
    return (hi_bits & jnp.uint32(0xFFFF0000)) | (lo_bits >> 16)


def _unpack_bf16_pair(word):
    lo = pltpu.bitcast(word << 16, F32)
    hi = pltpu.bitcast(word & jnp.uint32(0xFFFF0000), F32)
    return lo, hi


def _in_proj_kernel(x_ref, g_ref, w_ref, q_ref, k_ref, v_ref, u_ref):
    x = x_ref[...]
    ms = jnp.mean(x * x, axis=-1, keepdims=True)
    h = (x * lax.rsqrt(ms + NORM_EPS) * g_ref[...]).astype(BF16)
    proj = _dot(h, w_ref[...])
    scale = 1.0 / math.sqrt(SB_HEAD_DIM)
    q_ref[...] = (proj[:, :SB_WIDTH] * scale).astype(BF16)
    k_ref[...] = proj[:, SB_WIDTH:2 * SB_WIDTH].astype(BF16)
    v_ref[...] = proj[:, 2 * SB_WIDTH:3 * SB_WIDTH].astype(BF16)
    u_ref[...] = proj[:, 3 * SB_WIDTH:].astype(BF16)


def _in_proj(xf, g, w_bf16):
    n, d = xf.shape
    tm = min(PROJ_ROWS, n)
    wide = w_bf16.shape[1]
    out = jax.ShapeDtypeStruct((n, SB_WIDTH), BF16)
    row_spec = pl.BlockSpec((tm, SB_WIDTH), lambda i: (i, 0))
    return pl.pallas_call(
        _in_proj_kernel,
        grid=(n // tm,),
        in_specs=[pl.BlockSpec((tm, d), lambda i: (i, 0)),
                  pl.BlockSpec((1, d), lambda i: (0, 0)),
                  pl.BlockSpec((d, wide), lambda i: (0, 0))],
        out_specs=[row_spec, row_spec, row_spec, row_spec],
        out_shape=[out, out, out, jax.ShapeDtypeStruct((n, POOL_WIDTH), BF16)],
        compiler_params=pltpu.CompilerParams(
            dimension_semantics=("parallel",), vmem_limit_bytes=VMEM_LIMIT_BYTES),
        name="in_proj",
    )(xf, g, w_bf16)


def _attention_kernel(q_ref, k_ref, v_ref, g_ref, o_ref, acc_ref):
    tq = q_ref.shape[0]
    tk = tq
    qi = pl.program_id(2)
    lane = lax.broadcasted_iota(jnp.int32, (tq, LANES), 1)
    q = q_ref[...]
    zero = jnp.zeros_like(q)
    q_heads = (jnp.where(lane < SB_HEAD_DIM, q, zero), jnp.where(lane >= SB_HEAD_DIM, q, zero))

    row = lax.broadcasted_iota(jnp.int32, (tk, tk), 0)
    col = lax.broadcasted_iota(jnp.int32, (tk, tk), 1)
    later = (row > col).astype(BF16)
    past = col < row

    def visit(k_blk, v_blk, carries, diagonal):
        new_carries = []
        for h in range(2):
            z = _dot_nt(q_heads[h], k_blk)
            softplus = jnp.maximum(z, 0.0) + jnp.log(1.0 + jnp.exp(-jnp.abs(z)))
            log_keep = -softplus
            if diagonal:
                log_keep = jnp.where(past, log_keep, 0.0)
            hi, lo = _split_bf16(log_keep)
            after = _dot(hi, later) + _dot(lo, later) + carries[h]
            a = jnp.exp(z + log_keep + after)
            if diagonal:
                a = jnp.where(past, a, 0.0)
            acc_ref[h] += _dot(a.astype(BF16), v_blk)
            new_carries.append(carries[h] + jnp.sum(log_keep, axis=-1, keepdims=True))
        return tuple(new_carries)

    acc_ref[...] = jnp.zeros_like(acc_ref)
    start = pl.multiple_of(qi * tk, tk)
    zero_carry = jnp.zeros((tq, 1), F32)
    carries = visit(k_ref[pl.ds(start, tk), :], v_ref[pl.ds(start, tk), :],
                    (zero_carry, zero_carry), diagonal=True)

    def more(state):
        step, c0, c1 = state
        return (step < qi) & (jnp.maximum(jnp.max(c0), jnp.max(c1)) >= EXP_UNDERFLOW_F32)

    def body(state):
        step, c0, c1 = state
        off = pl.multiple_of((qi - 1 - step) * tk, tk)
        c0, c1 = visit(k_ref[pl.ds(off, tk), :], v_ref[pl.ds(off, tk), :], (c0, c1), diagonal=False)
        return step + 1, c0, c1

    lax.while_loop(more, body, (jnp.int32(0),) + carries)

    o = jnp.where(lane < SB_HEAD_DIM, acc_ref[0], acc_ref[1])
    sq = o * o
    s0 = jnp.sum(jnp.where(lane < SB_HEAD_DIM, sq, 0.0), axis=-1, keepdims=True)
    s1 = jnp.sum(jnp.where(lane >= SB_HEAD_DIM, sq, 0.0), axis=-1, keepdims=True)
    ms = jnp.where(lane < SB_HEAD_DIM, s0, s1) * (1.0 / SB_HEAD_DIM)
    o_ref[...] = (o * lax.rsqrt(ms + NORM_EPS) * g_ref[...]).astype(o_ref.dtype)


def _attention(q, k, v, sb_g):
    b, s, _ = q.shape
    t = min(ATTN_BLOCK, s)
    pairs = SB_WIDTH // LANES
    blk = pl.BlockSpec((None, t, LANES), lambda bi, p, i: (bi, i, p))
    seq = pl.BlockSpec((None, s, LANES), lambda bi, p, i: (bi, 0, p))
    return pl.pallas_call(
        _attention_kernel,
        grid=(b, pairs, s // t),
        in_specs=[blk, seq, seq, pl.BlockSpec((1, LANES), lambda bi, p, i: (0, p))],
        out_specs=blk,
        out_shape=jax.ShapeDtypeStruct((b, s, SB_WIDTH), BF16),
        scratch_shapes=[pltpu.VMEM((2, t, LANES), F32)],
        compiler_params=pltpu.CompilerParams(
            dimension_semantics=("parallel", "parallel", "arbitrary"),
            vmem_limit_bytes=VMEM_LIMIT_BYTES),
        name="attention",
    )(q, k, v, sb_g)


def _mix_route_kernel(x_ref, sb_ref, u_ref, halo_ref, wout_ref, wpool_ref, pscale_ref, g2_ref,
                      wr_ref, br_ref,
                      x1_ref, h2p_ref, idx_ref, gate_ref, rank_ref, count_ref,
                      count_acc, *, tiles_per_seq):
    tm = x_ref.shape[0]
    i = pl.program_id(0)
    seq_tile = i % tiles_per_seq

    t_main = lax.broadcasted_iota(jnp.int32, (tm, tm), 0)
    s_main = lax.broadcasted_iota(jnp.int32, (tm, tm), 1)
    t_halo = lax.broadcasted_iota(jnp.int32, (tm, POOL_HALO), 0)
    s_halo = lax.broadcasted_iota(jnp.int32, (tm, POOL_HALO), 1) - POOL_HALO
    pos = seq_tile * tm + lax.broadcasted_iota(jnp.int32, (tm, 1), 0)
    u = u_ref[...]
    halo = jnp.where(seq_tile > 0, halo_ref[...], jnp.zeros_like(halo_ref))
    y = _dot(sb_ref[...], wout_ref[pl.ds(0, SB_WIDTH), :])
    for g, w in enumerate(POOL_WINDOWS):
        cols = pl.ds(g * POOL_GROUP_DIM, POOL_GROUP_DIM)
        band_main = ((s_main <= t_main) & (s_main > t_main - w)).astype(BF16)
        band_halo = (s_halo > t_halo - w).astype(BF16)
        p = u[:, g * POOL_GROUP_DIM:(g + 1) * POOL_GROUP_DIM]
        win = _dot(band_main, p) + _dot(band_halo, halo[:, g * POOL_GROUP_DIM:(g + 1) * POOL_GROUP_DIM])
        cnt = jnp.minimum(pos + 1, w).astype(F32)
        mix = win / cnt - p.astype(F32)
        pooled = _dot(mix.astype(BF16), wpool_ref[g]) * pscale_ref[:, cols]
        y += _dot(pooled.astype(BF16), wout_ref[pl.ds(SB_WIDTH + g * POOL_GROUP_DIM, POOL_GROUP_DIM), :])

    x1 = x_ref[...] + y
    x1_ref[...] = x1
    ms = jnp.mean(x1 * x1, axis=-1, keepdims=True)
    h2 = x1 * lax.rsqrt(ms + NORM_EPS) * g2_ref[...]
    half = h2.shape[1] // 2
    h2p_ref[...] = _pack_bf16_pair(h2[:, :half], h2[:, half:])

    h_hi, h_lo = _split_bf16(h2)
    w_hi = wr_ref[0]
    w_lo = wr_ref[1]
    logits = _dot_nt(w_hi, h_hi) + _dot_nt(w_hi, h_lo) + _dot_nt(w_lo, h_hi) + br_ref[...]

    eidx = lax.broadcasted_iota(jnp.int32, (N_EXPERTS, tm), 0)
    work = logits
    vals, ids, hots = [], [], []
    for _ in range(TOP_K):
        m = jnp.max(work, axis=0, keepdims=True)
        sel = jnp.min(jnp.where(work == m, eidx, N_EXPERTS), axis=0, keepdims=True)
        hot = eidx == sel
        work = jnp.where(hot, -jnp.inf, work)
        vals.append(m)
        ids.append(sel)
        hots.append(hot)
    exps = [jnp.exp(v - vals[0]) for v in vals]
    denom = exps[0] + exps[1] + exps[2] + exps[3]

    @pl.when(i == 0)
    def _():
        count_acc[...] = jnp.zeros_like(count_acc)

    chosen = hots[0] | hots[1] | hots[2] | hots[3]
    earlier = (t_main < s_main).astype(BF16)
    before = _dot(chosen.astype(BF16), earlier) + count_acc[:, 0:1]
    for r in range(TOP_K):
        idx_ref[r:r + 1, :] = ids[r]
        gate_ref[r:r + 1, :] = exps[r] / denom
        rank = jnp.sum(jnp.where(hots[r], before, 0.0), axis=0, keepdims=True)
        rank_ref[r:r + 1, :] = rank.astype(jnp.int32)
    count_acc[...] += jnp.sum(chosen.astype(F32), axis=1, keepdims=True)
    count_ref[...] = count_acc[...]


def _mix_route(xf, sb, u, w_out, w_pool, pool_scale, g2, wr_split, b_router, seq_len):
    n, d = xf.shape
    tm = min(MIX_ROWS, seq_len)
    tiles_per_seq = seq_len // tm
    halo_blocks = tm // POOL_HALO
    row = lambda width: pl.BlockSpec((tm, width), lambda i: (i, 0))
    full = lambda shape: pl.BlockSpec(shape, lambda i: (0,) * len(shape))
    tok = pl.BlockSpec((TOP_K, tm), lambda i: (0, i))
    return pl.pallas_call(
        functools.partial(_mix_route_kernel, tiles_per_seq=tiles_per_seq),
        grid=(n // tm,),
        in_specs=[row(d), row(SB_WIDTH), row(POOL_WIDTH),
                  pl.BlockSpec((POOL_HALO, POOL_WIDTH), lambda i: (jnp.maximum(i * halo_blocks - 1, 0), 0)),
                  full(w_out.shape), full(w_pool.shape), full(pool_scale.shape), full(g2.shape),
                  full(wr_split.shape), full(b_router.shape)],
        out_specs=[row(d), row(d // 2), tok, tok, tok, full((N_EXPERTS, LANES))],
        out_shape=[jax.ShapeDtypeStruct((n, d), F32),
                   jax.ShapeDtypeStruct((n, d // 2), jnp.uint32),
                   jax.ShapeDtypeStruct((TOP_K, n), jnp.int32),
                   jax.ShapeDtypeStruct((TOP_K, n), F32),
                   jax.ShapeDtypeStruct((TOP_K, n), jnp.int32),
                   jax.ShapeDtypeStruct((N_EXPERTS, LANES), F32)],
        scratch_shapes=[pltpu.VMEM((N_EXPERTS, LANES), F32)],
        compiler_params=pltpu.CompilerParams(
            dimension_semantics=("arbitrary",), vmem_limit_bytes=VMEM_LIMIT_BYTES),
        name="mix_route",
    )(xf, sb, u, u, w_out, w_pool, pool_scale, g2, wr_split, b_router)


def _scatter_rows_kernel(dest_ref, h_ref, buf_in_ref, buf_ref, sem):
    del buf_in_ref
    tm = h_ref.shape[0]

    def copy(t, r):
        return pltpu.make_async_copy(h_ref.at[pl.ds(t, 1), :],
                                     buf_ref.at[pl.ds(dest_ref[r, t], 1), :], sem)

    def start(t, c):
        for r in range(TOP_K):
            copy(t, r).start(priority=r % 2)
        return c

    def wait(t, c):
        for r in range(TOP_K):
            copy(t, r).wait()
        return c

    lax.fori_loop(0, tm, start, 0, unroll=ROW_DMA_UNROLL)
    lax.fori_loop(0, tm, wait, 0, unroll=ROW_DMA_UNROLL)


def _scatter_rows(dest, h2p, n_rows):
    n, width = h2p.shape
    tm = min(SCATTER_ROWS, n)
    zeros = jnp.zeros((n_rows, width), h2p.dtype)
    return pl.pallas_call(
        _scatter_rows_kernel,
        grid=(n // tm,),
        in_specs=[pl.BlockSpec((TOP_K, tm), lambda i: (0, i), memory_space=pltpu.SMEM),
                  pl.BlockSpec((tm, width), lambda i: (i, 0)),
                  pl.BlockSpec(memory_space=pl.ANY)],
        out_specs=pl.BlockSpec(memory_space=pl.ANY),
        out_shape=jax.ShapeDtypeStruct((n_rows, width), h2p.dtype),
        scratch_shapes=[pltpu.SemaphoreType.DMA(())],
        input_output_aliases={2: 0},
        compiler_params=pltpu.CompilerParams(
            dimension_semantics=("arbitrary",), vmem_limit_bytes=VMEM_LIMIT_BYTES),
        name="scatter_rows",
    )(dest, h2p, zeros)


def _expert_ffn_kernel(block_e_ref, n_used_ref, x_ref, wgu_ref, bgu_ref, wd_ref, bd_ref, y_ref,
                       wgu_bf, wd_bf):
    i = pl.program_id(0)
    d_ff = wd_ref.shape[0]
    half = x_ref.shape[1]

    @pl.when(i < n_used_ref[0])
    def _():
        prev = block_e_ref[jnp.maximum(i - 1, 0)]

        @pl.when((i == 0) | (block_e_ref[i] != prev))
        def _():
            wgu_bf[...] = wgu_ref[...].astype(BF16)
            wd_bf[...] = wd_ref[...].astype(BF16)

        x_lo, x_hi = _unpack_bf16_pair(x_ref[...])
        gu = (_dot(x_lo.astype(BF16), wgu_bf[pl.ds(0, half), :])
              + _dot(x_hi.astype(BF16), wgu_bf[pl.ds(half, half), :]) + bgu_ref[...])
        gate = jnp.minimum(gu[:, :d_ff], SWIGLU_LIMIT)
        up = jnp.clip(gu[:, d_ff:], -SWIGLU_LIMIT, SWIGLU_LIMIT)
        act = (up + 1.0) * gate * (1.0 / (1.0 + jnp.exp(-SWIGLU_ALPHA * gate)))
        y = _dot(act.astype(BF16), wd_bf[...]) + bd_ref[...]
        y_ref[...] = _pack_bf16_pair(y[:, :half], y[:, half:])

    @pl.when(i >= n_used_ref[0])
    def _():
        y_ref[...] = jnp.zeros_like(y_ref)


def _expert_ffn(block_e, n_used, buf, w_gate_up, b_gate_up, w_down, b_down):
    n_rows, half = buf.shape
    _, d, two_f = w_gate_up.shape
    d_ff = w_down.shape[1]
    n_blocks = n_rows // FFN_ROWS

    def rows(i, be, nu):
        return (jnp.minimum(i, nu[0] - 1), 0)

    def expert(i, be, nu):
        return (be[jnp.minimum(i, nu[0] - 1)], 0, 0)

    grid_spec = pltpu.PrefetchScalarGridSpec(
        num_scalar_prefetch=2,
        grid=(n_blocks,),
        in_specs=[pl.BlockSpec((FFN_ROWS, half), rows),
                  pl.BlockSpec((None, d, two_f), expert),
                  pl.BlockSpec((None, 1, two_f), expert),
                  pl.BlockSpec((None, d_ff, d), expert),
                  pl.BlockSpec((None, 1, d), expert)],
        out_specs=pl.BlockSpec((FFN_ROWS, half), lambda i, be, nu: (i, 0)),
        scratch_shapes=[pltpu.VMEM((d, two_f), BF16), pltpu.VMEM((d_ff, d), BF16)],
    )
    return pl.pallas_call(
        _expert_ffn_kernel,
        grid_spec=grid_spec,
        out_shape=jax.ShapeDtypeStruct((n_rows, half), jnp.uint32),
        compiler_params=pltpu.CompilerParams(
            dimension_semantics=("arbitrary",), vmem_limit_bytes=VMEM_LIMIT_BYTES),
        name="expert_ffn",
    )(block_e, n_used, buf, w_gate_up, b_gate_up, w_down, b_down)


def _combine_kernel(dest_ref, x1_ref, gate_ref, g_ref, y_ref, o_ref, rows_ref, sem):
    tm = x1_ref.shape[0]

    def copy(t, r):
        return pltpu.make_async_copy(y_ref.at[pl.ds(dest_ref[r, t], 1), :],
                                     rows_ref.at[r, pl.ds(t, 1), :], sem)

    def start(t, c):
        for r in range(TOP_K):
            copy(t, r).start(priority=r % 2)
        return c

    def wait(t, c):
        for r in range(TOP_K):
            copy(t, r).wait()
        return c

    lax.fori_loop(0, tm, start, 0, unroll=ROW_DMA_UNROLL)
    lax.fori_loop(0, tm, wait, 0, unroll=ROW_DMA_UNROLL)

    gates = gate_ref[...]
    x1 = x1_ref[...]
    half = rows_ref.shape[2]
    lo_sum = x1[:, :half]
    hi_sum = x1[:, half:]
    for r in range(TOP_K):
        lo, hi = _unpack_bf16_pair(rows_ref[r])
        g = gates[:, r:r + 1]
        lo_sum += g * lo
        hi_sum += g * hi
    ms = (jnp.sum(lo_sum * lo_sum, axis=-1, keepdims=True)
          + jnp.sum(hi_sum * hi_sum, axis=-1, keepdims=True)) * (1.0 / (2 * half))
    inv = lax.rsqrt(ms + NORM_EPS)
    o_ref[:, :half] = lo_sum * inv * g_ref[:, :half]
    o_ref[:, half:] = hi_sum * inv * g_ref[:, half:]


def _combine(dest, x1, gates_t, final_g, y):
    n, d = x1.shape
    tm = min(COMBINE_ROWS, n)
    return pl.pallas_call(
        _combine_kernel,
        grid=(n // tm,),
        in_specs=[pl.BlockSpec((TOP_K, tm), lambda i: (0, i), memory_space=pltpu.SMEM),
                  pl.BlockSpec((tm, d), lambda i: (i, 0)),
                  pl.BlockSpec((tm, TOP_K), lambda i: (i, 0)),
                  pl.BlockSpec((1, d), lambda i: (0, 0)),
                  pl.BlockSpec(memory_space=pl.ANY)],
        out_specs=pl.BlockSpec((tm, d), lambda i: (i, 0)),
        out_shape=jax.ShapeDtypeStruct((n, d), F32),
        scratch_shapes=[pltpu.VMEM((TOP_K, tm, d // 2), jnp.uint32), pltpu.SemaphoreType.DMA(())],
        compiler_params=pltpu.CompilerParams(
            dimension_semantics=("arbitrary",), vmem_limit_bytes=VMEM_LIMIT_BYTES),
        name="combine",
    )(dest, x1, gates_t, final_g, y)


def _layer(x, norm1_g, w_in, sb_norm_g, w_pool, pool_scale, w_out, norm2_g,
           w_router, b_router, w_gate_up, b_gate_up, w_down, b_down, out_g):
    b, s, d = x.shape
    n = b * s
    xf = x.reshape(n, d)

    q, k, v, u = _in_proj(xf, norm1_g.reshape(1, d), w_in.astype(BF16))
    sb = _attention(q.reshape(b, s, SB_WIDTH), k.reshape(b, s, SB_WIDTH), v.reshape(b, s, SB_WIDTH),
                    sb_norm_g.reshape(1, SB_WIDTH))

    wr_t = w_router.T
    wr_hi = wr_t.astype(BF16)
    wr_lo = (wr_t - wr_hi.astype(F32)).astype(BF16)
    x1, h2p, idx, gates, rank, counts = _mix_route(
        xf, sb.reshape(n, SB_WIDTH), u, w_out.astype(BF16), w_pool.astype(BF16),
        pool_scale.reshape(1, POOL_WIDTH), norm2_g.reshape(1, d),
        jnp.stack([wr_hi, wr_lo]), b_router.reshape(N_EXPERTS, 1), s)

    counts = counts[:, 0].astype(jnp.int32)
    padded = (counts + FFN_ROWS - 1) // FFN_ROWS * FFN_ROWS
    pend = jnp.cumsum(padded)
    pstart = pend - padded
    experts = jnp.arange(N_EXPERTS, dtype=jnp.int32)[:, None, None]
    dest = rank + jnp.sum(jnp.where(idx[None] == experts, pstart[:, None, None], 0), axis=0)
    n_blocks = -(-(n * TOP_K) // FFN_ROWS) + N_EXPERTS
    n_used = (pend[-1] // FFN_ROWS).astype(jnp.int32).reshape(1)
    block_start = jnp.arange(n_blocks, dtype=jnp.int32) * FFN_ROWS
    block_e = jnp.minimum(jnp.sum(pend[None, :] <= block_start[:, None], axis=1),
                          N_EXPERTS - 1).astype(jnp.int32)

    buf = _scatter_rows(dest, h2p, n_blocks * FFN_ROWS)
    y = _expert_ffn(block_e, n_used, buf, w_gate_up, b_gate_up.reshape(N_EXPERTS, 1, -1),
                    w_down, b_down.reshape(N_EXPERTS, 1, -1))
    out = _combine(dest, x1, gates.T, out_g.reshape(1, d), y)
    return out.reshape(b, s, d)


def kernel(x, norm1_g, w_in, sb_norm_g, w_pool, pool_scale, w_out, norm2_g, w_router, b_router,
           w_gate_up, b_gate_up, w_down, b_down, final_norm_g):
    assert norm1_g.shape[0] == 1, "single-layer block"
    return _layer(x, norm1_g[0], w_in[0], sb_norm_g[0], w_pool[0], pool_scale[0], w_out[0],
                  norm2_g[0], w_router[0], b_router[0], w_gate_up[0], b_gate_up[0], w_down[0],
                  b_down[0], final_norm_g)
```

```python
import functools
import math

import jax
import jax.numpy as jnp
from jax import lax
from jax.experimental import pallas as pl
from jax.experimental.pallas import tpu as pltpu

NORM_EPS = 1e-5
SB_HEADS = 8
SB_HEAD_DIM = 64
SB_WIDTH = SB_HEADS * SB_HEAD_DIM
POOL_WINDOWS = (2, 4, 8, 16)
POOL_GROUP_DIM = 128
POOL_WIDTH = len(POOL_WINDOWS) * POOL_GROUP_DIM
POOL_HALO = 16
N_EXPERTS = 32
TOP_K = 4
SWIGLU_LIMIT = 7.0
SWIGLU_ALPHA = 1.702
EXP_UNDERFLOW_F32 = -105.0

LANES = 128
VMEM_LIMIT_BYTES = 56 * 1024 * 1024

PROJ_ROWS = 512
ATTN_BLOCK = 256
ATTN_CHUNK = 128
MIX_ROWS = 512
ROW_DMA_UNROLL = 4
SCATTER_ROWS = 256
FFN_ROWS = 512
COMBINE_ROWS = 256

F32 = jnp.float32
BF16 = jnp.bfloat16


def _dot(a, b):
    return jnp.dot(a, b, preferred_element_type=F32)


def _dot_nt(a, b):
    return lax.dot_general(a, b, (((1,), (1,)), ((), ())), preferred_element_type=F32)


def _split_bf16(x):
    hi = x.astype(BF16)
    lo = (x - hi.astype(F32)).astype(BF16)
    return hi, lo


def _pack_bf16_pair(lo_f32, hi_f32):
    lo_bits = pltpu.bitcast(lo_f32.astype(BF16).astype(F32), jnp.uint32)
    hi_bits = pltpu.bitcast(hi_f32.astype(BF16).astype(F32), jnp.uint32)
    return (hi_bits & jnp.uint32(0xFFFF0000)) | (lo_bits >> 16)


def _unpack_bf16_pair(word):
    lo = pltpu.bitcast(word << 16, F32)
    hi = pltpu.bitcast(word & jnp.uint32(0xFFFF0000), F32)
    return lo, hi


def _in_proj_kernel(x_ref, g_ref, w_ref, q_ref, k_ref, v_ref, u_ref):
    x = x_ref[...]
    ms = jnp.mean(x * x, axis=-1, keepdims=True)
    h = (x * lax.rsqrt(ms + NORM_EPS) * g_ref[...]).astype(BF16)
    proj = _dot(h, w_ref[...])
    scale = 1.0 / math.sqrt(SB_HEAD_DIM)
    q_ref[...] = (proj[:, :SB_WIDTH] * scale).astype(BF16)
    k_ref[...] = proj[:, SB_WIDTH:2 * SB_WIDTH].astype(BF16)
    v_ref[...] = proj[:, 2 * SB_WIDTH:3 * SB_WIDTH].astype(BF16)
    u_ref[...] = proj[:, 3 * SB_WIDTH:].astype(BF16)


def _in_proj(xf, g, w_bf16):
    n, d = xf.shape
    tm = min(PROJ_ROWS, n)
    wide = w_bf16.shape[1]
    out = jax.ShapeDtypeStruct((n, SB_WIDTH), BF16)
    row_spec = pl.BlockSpec((tm, SB_WIDTH), lambda i: (i, 0))
    return pl.pallas_call(
        _in_proj_kernel,
        grid=(n // tm,),
        in_specs=[pl.BlockSpec((tm, d), lambda i: (i, 0)),
                  pl.BlockSpec((1, d), lambda i: (0, 0)),
                  pl.BlockSpec((d, wide), lambda i: (0, 0))],
        out_specs=[row_spec, row_spec, row_spec, row_spec],
        out_shape=[out, out, out, jax.ShapeDtypeStruct((n, POOL_WIDTH), BF16)],
        compiler_params=pltpu.CompilerParams(
            dimension_semantics=("parallel",), vmem_limit_bytes=VMEM_LIMIT_BYTES),
        name="in_proj",
    )(xf, g, w_bf16)


def _attention_kernel(q_ref, k_ref, v_ref, g_ref, o_ref, acc_ref, carry_ref):
    tq = q_ref.shape[0]
    tk = tq
    rows = 2 * tq
    n_chunks = rows // ATTN_CHUNK
    qi = pl.program_id(2)
    lane = lax.broadcasted_iota(jnp.int32, (tq, LANES), 1)
    q = q_ref[...].astype(F32)
    q_stack = jnp.concatenate([jnp.where(lane < SB_HEAD_DIM, q, 0.0),
                               jnp.where(lane >= SB_HEAD_DIM, q, 0.0)], axis=0).astype(BF16)

    key_row = lax.broadcasted_iota(jnp.int32, (tk, tk), 0)
    key_col = lax.broadcasted_iota(jnp.int32, (tk, tk), 1)
    later = (key_row > key_col).astype(BF16)

    chunks = [slice(c * ATTN_CHUNK, (c + 1) * ATTN_CHUNK) for c in range(n_chunks)]

    def past_mask(c):
        query = (c.start % tq) + lax.broadcasted_iota(jnp.int32, (ATTN_CHUNK, tk), 0)
        return lax.broadcasted_iota(jnp.int32, (ATTN_CHUNK, tk), 1) < query

    def visit(offsets, carry, first_is_diagonal):
        blocks = [(k_ref[pl.ds(off, tk), :], v_ref[pl.ds(off, tk), :]) for off in offsets]
        chains = [(b, c) for b in range(len(blocks)) for c in chunks]
        masked = lambda b: first_is_diagonal and b == 0
        z = [_dot_nt(q_stack[c], blocks[b][0]) for b, c in chains]
        log_keep, split = [], []
        for (b, c), zc in zip(chains, z):
            lk = -(jnp.maximum(zc, 0.0) + jnp.log(1.0 + jnp.exp(-jnp.abs(zc))))
            if masked(b):
                lk = jnp.where(past_mask(c), lk, 0.0)
            log_keep.append(lk)
            split.append(jnp.concatenate(_split_bf16(lk), axis=0))
        sums = [_dot(sp, later) for sp in split]
        carries = {c.start: (None if carry is None else carry[c]) for c in chunks}
        for i, (b, c) in enumerate(chains):
            after = sums[i][:ATTN_CHUNK] + sums[i][ATTN_CHUNK:]
            if carries[c.start] is not None:
                after = after + carries[c.start]
            a = jnp.exp(z[i] + log_keep[i] + after)
            if masked(b):
                a = jnp.where(past_mask(c), a, 0.0)
            acc_ref[c, :] += _dot(a.astype(BF16), blocks[b][1])
            total = jnp.sum(log_keep[i], axis=-1, keepdims=True)
            carries[c.start] = total if carries[c.start] is None else carries[c.start] + total
        return jnp.concatenate([carries[c.start] for c in chunks], axis=0)

    acc_ref[...] = jnp.zeros_like(acc_ref)
    start = pl.multiple_of(qi * tk, tk)

    @pl.when(qi == 0)
    def _():
        carry_ref[...] = visit([start], None, first_is_diagonal=True)

    @pl.when(qi > 0)
    def _():
        carry_ref[...] = visit([start, pl.multiple_of(start - tk, tk)], None, first_is_diagonal=True)

    def more(state):
        step, carry = state
        return (step < qi) & (jnp.max(carry) >= EXP_UNDERFLOW_F32)

    def body(state):
        step, carry = state
        off = pl.multiple_of((qi - 1 - step) * tk, tk)
        return step + 1, visit([off], carry, first_is_diagonal=False)

    lax.while_loop(more, body, (jnp.minimum(qi, 1), carry_ref[...]))

    o = jnp.where(lane < SB_HEAD_DIM, acc_ref[pl.ds(0, tq), :], acc_ref[pl.ds(tq, tq), :])
    sq = o * o
    s0 = jnp.sum(jnp.where(lane < SB_HEAD_DIM, sq, 0.0), axis=-1, keepdims=True)
    s1 = jnp.sum(jnp.where(lane >= SB_HEAD_DIM, sq, 0.0), axis=-1, keepdims=True)
    ms = jnp.where(lane < SB_HEAD_DIM, s0, s1) * (1.0 / SB_HEAD_DIM)
    o_ref[...] = (o * lax.rsqrt(ms + NORM_EPS) * g_ref[...]).astype(o_ref.dtype)


def _attention(q, k, v, sb_g):
    b, s, _ = q.shape
    t = min(ATTN_BLOCK, s)
    pairs = SB_WIDTH // LANES
    blk = pl.BlockSpec((None, t, LANES), lambda bi, p, i: (bi, i, p))
    seq = pl.BlockSpec((None, s, LANES), lambda bi, p, i: (bi, 0, p))
    return pl.pallas_call(
        _attention_kernel,
        grid=(b, pairs, s // t),
        in_specs=[blk, seq, seq, pl.BlockSpec((1, LANES), lambda bi, p, i: (0, p))],
        out_specs=blk,
        out_shape=jax.ShapeDtypeStruct((b, s, SB_WIDTH), BF16),
        scratch_shapes=[pltpu.VMEM((2 * t, LANES), F32), pltpu.VMEM((2 * t, 1), F32)],
        compiler_params=pltpu.CompilerParams(
            dimension_semantics=("parallel", "parallel", "arbitrary"),
            vmem_limit_bytes=VMEM_LIMIT_BYTES),
        name="attention",
    )(q, k, v, sb_g)


def _mix_route_kernel(x_ref, sb_ref, u_ref, halo_ref, wout_ref, wpool_ref, pscale_ref, g2_ref,
                      wr_ref, br_ref,
                      x1_ref, h2p_ref, idx_ref, gate_ref, rank_ref, count_ref,
                      count_acc, *, tiles_per_seq):
    tm = x_ref.shape[0]
    i = pl.program_id(0)
    seq_tile = i % tiles_per_seq

    t_main = lax.broadcasted_iota(jnp.int32, (tm, tm), 0)
    s_main = lax.broadcasted_iota(jnp.int32, (tm, tm), 1)
    t_halo = lax.broadcasted_iota(jnp.int32, (tm, POOL_HALO), 0)
    s_halo = lax.broadcasted_iota(jnp.int32, (tm, POOL_HALO), 1) - POOL_HALO
    pos = seq_tile * tm + lax.broadcasted_iota(jnp.int32, (tm, 1), 0)
    u = u_ref[...]
    halo = jnp.where(seq_tile > 0, halo_ref[...], jnp.zeros_like(halo_ref))
    y = _dot(sb_ref[...], wout_ref[pl.ds(0, SB_WIDTH), :])
    for g, w in enumerate(POOL_WINDOWS):
        cols = pl.ds(g * POOL_GROUP_DIM, POOL_GROUP_DIM)
        band_main = ((s_main <= t_main) & (s_main > t_main - w)).astype(BF16)
        band_halo = (s_halo > t_halo - w).astype(BF16)
        p = u[:, g * POOL_GROUP_DIM:(g + 1) * POOL_GROUP_DIM]
        win = _dot(band_main, p) + _dot(band_halo, halo[:, g * POOL_GROUP_DIM:(g + 1) * POOL_GROUP_DIM])
        cnt = jnp.minimum(pos + 1, w).astype(F32)
        mix = win / cnt - p.astype(F32)
        pooled = _dot(mix.astype(BF16), wpool_ref[g]) * pscale_ref[:, cols]
        y += _dot(pooled.astype(BF16), wout_ref[pl.ds(SB_WIDTH + g * POOL_GROUP_DIM, POOL_GROUP_DIM), :])

    x1 = x_ref[...] + y
    x1_ref[...] = x1
    ms = jnp.mean(x1 * x1, axis=-1, keepdims=True)
    h2 = x1 * lax.rsqrt(ms + NORM_EPS) * g2_ref[...]
    half = h2.shape[1] // 2
    h2p_ref[...] = _pack_bf16_pair(h2[:, :half], h2[:, half:])

    h_hi, h_lo = _split_bf16(h2)
    w_hi = wr_ref[0]
    w_lo = wr_ref[1]
    logits = _dot_nt(w_hi, h_hi) + _dot_nt(w_hi, h_lo) + _dot_nt(w_lo, h_hi) + br_ref[...]

    eidx = lax.broadcasted_iota(jnp.int32, (N_EXPERTS, tm), 0)
    work = logits
    vals, ids, hots = [], [], []
    for _ in range(TOP_K):
        m = jnp.max(work, axis=0, keepdims=True)
        sel = jnp.min(jnp.where(work == m, eidx, N_EXPERTS), axis=0, keepdims=True)
        hot = eidx == sel
        work = jnp.where(hot, -jnp.inf, work)
        vals.append(m)
        ids.append(sel)
        hots.append(hot)
    exps = [jnp.exp(v - vals[0]) for v in vals]
    denom = exps[0] + exps[1] + exps[2] + exps[3]

    @pl.when(i == 0)
    def _():
        count_acc[...] = jnp.zeros_like(count_acc)

    chosen = hots[0] | hots[1] | hots[2] | hots[3]
    earlier = (t_main < s_main).astype(BF16)
    before = _dot(chosen.astype(BF16), earlier) + count_acc[:, 0:1]
    for r in range(TOP_K):
        idx_ref[r:r + 1, :] = ids[r]
        gate_ref[r:r + 1, :] = exps[r] / denom
        rank = jnp.sum(jnp.where(hots[r], before, 0.0), axis=0, keepdims=True)
        rank_ref[r:r + 1, :] = rank.astype(jnp.int32)
    count_acc[...] += jnp.sum(chosen.astype(F32), axis=1, keepdims=True)
    count_ref[...] = count_acc[...]


def _mix_route(xf, sb, u, w_out, w_pool, pool_scale, g2, wr_split, b_router, seq_len):
    n, d = xf.shape
    tm = min(MIX_ROWS, seq_len)
    tiles_per_seq = seq_len // tm
    halo_blocks = tm // POOL_HALO
    row = lambda width: pl.BlockSpec((tm, width), lambda i: (i, 0))
    full = lambda shape: pl.BlockSpec(shape, lambda i: (0,) * len(shape))
    tok = pl.BlockSpec((TOP_K, tm), lambda i: (0, i))
    return pl.pallas_call(
        functools.partial(_mix_route_kernel, tiles_per_seq=tiles_per_seq),
        grid=(n // tm,),
        in_specs=[row(d), row(SB_WIDTH), row(POOL_WIDTH),
                  pl.BlockSpec((POOL_HALO, POOL_WIDTH), lambda i: (jnp.maximum(i * halo_blocks - 1, 0), 0)),
                  full(w_out.shape), full(w_pool.shape), full(pool_scale.shape), full(g2.shape),
                  full(wr_split.shape), full(b_router.shape)],
        out_specs=[row(d), row(d // 2), tok, tok, tok, full((N_EXPERTS, LANES))],
        out_shape=[jax.ShapeDtypeStruct((n, d), F32),
                   jax.ShapeDtypeStruct((n, d // 2), jnp.uint32),
                   jax.ShapeDtypeStruct((TOP_K, n), jnp.int32),
                   jax.ShapeDtypeStruct((TOP_K, n), F32),
                   jax.ShapeDtypeStruct((TOP_K, n), jnp.int32),
                   jax.ShapeDtypeStruct((N_EXPERTS, LANES), F32)],
        scratch_shapes=[pltpu.VMEM((N_EXPERTS, LANES), F32)],
        compiler_params=pltpu.CompilerParams(
            dimension_semantics=("arbitrary",), vmem_limit_bytes=VMEM_LIMIT_BYTES),
        name="mix_route",
    )(xf, sb, u, u, w_out, w_pool, pool_scale, g2, wr_split, b_router)


def _scatter_rows_kernel(dest_ref, h_ref, buf_in_ref, buf_ref, sem):
    del buf_in_ref
    tm = h_ref.shape[0]

    def copy(t, r):
        return pltpu.make_async_copy(h_ref.at[pl.ds(t, 1), :],
                                     buf_ref.at[pl.ds(dest_ref[r, t], 1), :], sem)

    def start(t, c):
        for r in range(TOP_K):
            copy(t, r).start(priority=r % 2)
        return c

    def wait(t, c):
        for r in range(TOP_K):
            copy(t, r).wait()
        return c

    lax.fori_loop(0, tm, start, 0, unroll=ROW_DMA_UNROLL)
    lax.fori_loop(0, tm, wait, 0, unroll=ROW_DMA_UNROLL)


def _scatter_rows(dest, h2p, n_rows):
    n, width = h2p.shape
    tm = min(SCATTER_ROWS, n)
    zeros = jnp.zeros((n_rows, width), h2p.dtype)
    return pl.pallas_call(
        _scatter_rows_kernel,
        grid=(n // tm,),
        in_specs=[pl.BlockSpec((TOP_K, tm), lambda i: (0, i), memory_space=pltpu.SMEM),
                  pl.BlockSpec((tm, width), lambda i: (i, 0)),
                  pl.BlockSpec(memory_space=pl.ANY)],
        out_specs=pl.BlockSpec(memory_space=pl.ANY),
        out_shape=jax.ShapeDtypeStruct((n_rows, width), h2p.dtype),
        scratch_shapes=[pltpu.SemaphoreType.DMA(())],
        input_output_aliases={2: 0},
        compiler_params=pltpu.CompilerParams(
            dimension_semantics=("arbitrary",), vmem_limit_bytes=VMEM_LIMIT_BYTES),
        name="scatter_rows",
    )(dest, h2p, zeros)


def _expert_ffn_kernel(block_e_ref, n_used_ref, x_ref, wgu_ref, bgu_ref, wd_ref, bd_ref, y_ref,
                       wgu_bf, wd_bf):
    i = pl.program_id(0)
    d_ff = wd_ref.shape[0]
    half = x_ref.shape[1]

    @pl.when(i < n_used_ref[0])
    def _():
        prev = block_e_ref[jnp.maximum(i - 1, 0)]

        @pl.when((i == 0) | (block_e_ref[i] != prev))
        def _():
            wgu_bf[...] = wgu_ref[...].astype(BF16)
            wd_bf[...] = wd_ref[...].astype(BF16)

        x_lo, x_hi = _unpack_bf16_pair(x_ref[...])
        gu = (_dot(x_lo.astype(BF16), wgu_bf[pl.ds(0, half), :])
              + _dot(x_hi.astype(BF16), wgu_bf[pl.ds(half, half), :]) + bgu_ref[...])
        gate = jnp.minimum(gu[:, :d_ff], SWIGLU_LIMIT)
        up = jnp.clip(gu[:, d_ff:], -SWIGLU_LIMIT, SWIGLU_LIMIT)
        act = (up + 1.0) * gate * (1.0 / (1.0 + jnp.exp(-SWIGLU_ALPHA * gate)))
        y = _dot(act.astype(BF16), wd_bf[...]) + bd_ref[...]
        y_ref[...] = _pack_bf16_pair(y[:, :half], y[:, half:])

    @pl.when(i >= n_used_ref[0])
    def _():
        y_ref[...] = jnp.zeros_like(y_ref)


def _expert_ffn(block_e, n_used, buf, w_gate_up, b_gate_up, w_down, b_down):
    n_rows, half = buf.shape
    _, d, two_f = w_gate_up.shape
    d_ff = w_down.shape[1]
    n_blocks = n_rows // FFN_ROWS

    def rows(i, be, nu):
        return (jnp.minimum(i, nu[0] - 1), 0)

    def expert(i, be, nu):
        return (be[jnp.minimum(i, nu[0] - 1)], 0, 0)

    grid_spec = pltpu.PrefetchScalarGridSpec(
        num_scalar_prefetch=2,
        grid=(n_blocks,),
        in_specs=[pl.BlockSpec((FFN_ROWS, half), rows),
                  pl.BlockSpec((None, d, two_f), expert),
                  pl.BlockSpec((None, 1, two_f), expert),
                  pl.BlockSpec((None, d_ff, d), expert),
                  pl.BlockSpec((None, 1, d), expert)],
        out_specs=pl.BlockSpec((FFN_ROWS, half), lambda i, be, nu: (i, 0)),
        scratch_shapes=[pltpu.VMEM((d, two_f), BF16), pltpu.VMEM((d_ff, d), BF16)],
    )
    return pl.pallas_call(
        _expert_ffn_kernel,
        grid_spec=grid_spec,
        out_shape=jax.ShapeDtypeStruct((n_rows, half), jnp.uint32),
        compiler_params=pltpu.CompilerParams(
            dimension_semantics=("arbitrary",), vmem_limit_bytes=VMEM_LIMIT_BYTES),
        name="expert_ffn",
    )(block_e, n_used, buf, w_gate_up, b_gate_up, w_down, b_down)


def _combine_kernel(dest_ref, x1_ref, gate_ref, g_ref, y_ref, o_ref, rows_ref, sem):
    tm = x1_ref.shape[0]

    def copy(t, r):
        return pltpu.make_async_copy(y_ref.at[pl.ds(dest_ref[r, t], 1), :],
                                     rows_ref.at[r, pl.ds(t, 1), :], sem)

    def start(t, c):
        for r in range(TOP_K):
            copy(t, r).start(priority=r % 2)
        return c

    def wait(t, c):
        for r in range(TOP_K):
            copy(t, r).wait()
        return c

    lax.fori_loop(0, tm, start, 0, unroll=ROW_DMA_UNROLL)
    lax.fori_loop(0, tm, wait, 0, unroll=ROW_DMA_UNROLL)

    gates = gate_ref[...]
    x1 = x1_ref[...]
    half = rows_ref.shape[2]
    lo_sum = x1[:, :half]
    hi_sum = x1[:, half:]
    for r in range(TOP_K):
        lo, hi = _unpack_bf16_pair(rows_ref[r])
        g = gates[:, r:r + 1]
        lo_sum += g * lo
        hi_sum += g * hi
    ms = (jnp.sum(lo_sum * lo_sum, axis=-1, keepdims=True)
          + jnp.sum(hi_sum * hi_sum, axis=-1, keepdims=True)) * (1.0 / (2 * half))
    inv = lax.rsqrt(ms + NORM_EPS)
    o_ref[:, :half] = lo_sum * inv * g_ref[:, :half]
    o_ref[:, half:] = hi_sum * inv * g_ref[:, half:]


def _combine(dest, x1, gates_t, final_g, y):
    n, d = x1.shape
    tm = min(COMBINE_ROWS, n)
    return pl.pallas_call(
        _combine_kernel,
        grid=(n // tm,),
        in_specs=[pl.BlockSpec((TOP_K, tm), lambda i: (0, i), memory_space=pltpu.SMEM),
                  pl.BlockSpec((tm, d), lambda i: (i, 0)),
                  pl.BlockSpec((tm, TOP_K), lambda i: (i, 0)),
                  pl.BlockSpec((1, d), lambda i: (0, 0)),
                  pl.BlockSpec(memory_space=pl.ANY)],
        out_specs=pl.BlockSpec((tm, d), lambda i: (i, 0)),
        out_shape=jax.ShapeDtypeStruct((n, d), F32),
        scratch_shapes=[pltpu.VMEM((TOP_K, tm, d // 2), jnp.uint32), pltpu.SemaphoreType.DMA(())],
        compiler_params=pltpu.CompilerParams(
            dimension_semantics=("arbitrary",), vmem_limit_bytes=VMEM_LIMIT_BYTES),
        name="combine",
    )(dest, x1, gates_t, final_g, y)


def _layer(x, norm1_g, w_in, sb_norm_g, w_pool, pool_scale, w_out, norm2_g,
           w_router, b_router, w_gate_up, b_gate_up, w_down, b_down, out_g):
    b, s, d = x.shape
    n = b * s
    xf = x.reshape(n, d)

    q, k, v, u = _in_proj(xf, norm1_g.reshape(1, d), w_in.astype(BF16))
    sb = _attention(q.reshape(b, s, SB_WIDTH), k.reshape(b, s, SB_WIDTH), v.reshape(b, s, SB_WIDTH),
                    sb_norm_g.reshape(1, SB_WIDTH))

    wr_t = w_router.T
    wr_hi = wr_t.astype(BF16)
    wr_lo = (wr_t - wr_hi.astype(F32)).astype(BF16)
    x1, h2p, idx, gates, rank, counts = _mix_route(
        xf, sb.reshape(n, SB_WIDTH), u, w_out.astype(BF16), w_pool.astype(BF16),
        pool_scale.reshape(1, POOL_WIDTH), norm2_g.reshape(1, d),
        jnp.stack([wr_hi, wr_lo]), b_router.reshape(N_EXPERTS, 1), s)

    counts = counts[:, 0].astype(jnp.int32)
    padded = (counts + FFN_ROWS - 1) // FFN_ROWS * FFN_ROWS
    pend = jnp.cumsum(padded)
    pstart = pend - padded
    experts = jnp.arange(N_EXPERTS, dtype=jnp.int32)[:, None, None]
    dest = rank + jnp.sum(jnp.where(idx[None] == experts, pstart[:, None, None], 0), axis=0)
    n_blocks = -(-(n * TOP_K) // FFN_ROWS) + N_EXPERTS
    n_used = (pend[-1] // FFN_ROWS).astype(jnp.int32).reshape(1)
    block_start = jnp.arange(n_blocks, dtype=jnp.int32) * FFN_ROWS
    block_e = jnp.minimum(jnp.sum(pend[None, :] <= block_start[:, None], axis=1),
                          N_EXPERTS - 1).astype(jnp.int32)

    buf = _scatter_rows(dest, h2p, n_blocks * FFN_ROWS)
    y = _expert_ffn(block_e, n_used, buf, w_gate_up, b_gate_up.reshape(N_EXPERTS, 1, -1),
                    w_down, b_down.reshape(N_EXPERTS, 1, -1))
    out = _combine(dest, x1, gates.T, out_g.reshape(1, d), y)
    return out.reshape(b, s, d)


def kernel(x, norm1_g, w_in, sb_norm_g, w_pool, pool_scale, w_out, norm2_g, w_router, b_router,
           w_gate_up, b_gate_up, w_down, b_down, final_norm_g):
    assert norm1_g.shape[0] == 1, "single-layer block"
    return _layer(x, norm1_g[0], w_in[0], sb_norm_g[0], w_pool[0], pool_scale[0], w_out[0],
                  norm2_g[0], w_router[0], b_router[0], w_gate_up[0], b_gate_up[0], w_down[0],
                  b_down[0], final_norm_g)
```

```python
import functools
import math

import jax
import jax.numpy as jnp
from jax import lax
from jax.experimental import pallas as pl
from jax.experimental.pallas import tpu as pltpu

NORM_EPS = 1e-5
SB_HEADS = 8
SB_HEAD_DIM = 64
SB_WIDTH = SB_HEADS * SB_HEAD_DIM
POOL_WINDOWS = (2, 4, 8, 16)
POOL_GROUP_DIM = 128
POOL_WIDTH = len(POOL_WINDOWS) * POOL_GROUP_DIM
POOL_HALO = 16
N_EXPERTS = 32
TOP_K = 4
SWIGLU_LIMIT = 7.0
SWIGLU_ALPHA = 1.702
EXP_UNDERFLOW_F32 = -105.0

LANES = 128
VMEM_LIMIT_BYTES = 56 * 1024 * 1024

PROJ_ROWS = 512
ATTN_BLOCK = 256
ATTN_CHUNK = 128
ATTN_PAIRS = 2
MIX_ROWS = 512
ROW_DMA_GROUP = 8
SCATTER_ROWS = 256
FFN_ROWS = 512
COMBINE_ROWS = 256

F32 = jnp.float32
BF16 = jnp.bfloat16


def _dot(a, b):
    return jnp.dot(a, b, preferred_element_type=F32)


def _dot_nt(a, b):
    return lax.dot_general(a, b, (((1,), (1,)), ((), ())), preferred_element_type=F32)


def _split_bf16(x):
    hi = x.astype(BF16)
    lo = (x - hi.astype(F32)).astype(BF16)
    return hi, lo


def _pack_bf16_pair(lo_f32, hi_f32):
    lo_bits = pltpu.bitcast(lo_f32.astype(BF16).astype(F32), jnp.uint32)
    hi_bits = pltpu.bitcast(hi_f32.astype(BF16).astype(F32), jnp.uint32)
    return (hi_bits & jnp.uint32(0xFFFF0000)) | (lo_bits >> 16)


def _unpack_bf16_pair(word):
    lo = pltpu.bitcast(word << 16, F32)
    hi = pltpu.bitcast(word & jnp.uint32(0xFFFF0000), F32)
    return lo, hi


def _in_proj_kernel(x_ref, g_ref, w_ref, q_ref, k_ref, v_ref, u_ref):
    x = x_ref[...]
    ms = jnp.mean(x * x, axis=-1, keepdims=True)
    h = (x * lax.rsqrt(ms + NORM_EPS) * g_ref[...]).astype(BF16)
    proj = _dot(h, w_ref[...])
    scale = 1.0 / math.sqrt(SB_HEAD_DIM)
    q_ref[...] = (proj[:, :SB_WIDTH] * scale).astype(BF16)
    k_ref[...] = proj[:, SB_WIDTH:2 * SB_WIDTH].astype(BF16)
    v_ref[...] = proj[:, 2 * SB_WIDTH:3 * SB_WIDTH].astype(BF16)
    u_ref[...] = proj[:, 3 * SB_WIDTH:].astype(BF16)


def _in_proj(xf, g, w_bf16):
    n, d = xf.shape
    tm = min(PROJ_ROWS, n)
    wide = w_bf16.shape[1]
    out = jax.ShapeDtypeStruct((n, SB_WIDTH), BF16)
    row_spec = pl.BlockSpec((tm, SB_WIDTH), lambda i: (i, 0))
    return pl.pallas_call(
        _in_proj_kernel,
        grid=(n // tm,),
        in_specs=[pl.BlockSpec((tm, d), lambda i: (i, 0)),
                  pl.BlockSpec((1, d), lambda i: (0, 0)),
                  pl.BlockSpec((d, wide), lambda i: (0, 0))],
        out_specs=[row_spec, row_spec, row_spec, row_spec],
        out_shape=[out, out, out, jax.ShapeDtypeStruct((n, POOL_WIDTH), BF16)],
        compiler_params=pltpu.CompilerParams(
            dimension_semantics=("parallel",), vmem_limit_bytes=VMEM_LIMIT_BYTES),
        name="in_proj",
    )(xf, g, w_bf16)


def _attention_kernel(q_ref, k_ref, v_ref, g_ref, o_ref, acc_ref, carry_ref):
    tq = q_ref.shape[0]
    tk = tq
    n_pairs = q_ref.shape[1] // LANES
    rows = 2 * n_pairs * tq
    n_chunks = rows // ATTN_CHUNK
    qi = pl.program_id(2)
    lane = lax.broadcasted_iota(jnp.int32, (tq, LANES), 1)
    stacked = []
    for p in range(n_pairs):
        q = q_ref[:, pl.ds(p * LANES, LANES)].astype(F32)
        stacked += [jnp.where(lane < SB_HEAD_DIM, q, 0.0), jnp.where(lane >= SB_HEAD_DIM, q, 0.0)]
    q_stack = jnp.concatenate(stacked, axis=0).astype(BF16)

    key_row = lax.broadcasted_iota(jnp.int32, (tk, tk), 0)
    key_col = lax.broadcasted_iota(jnp.int32, (tk, tk), 1)
    later = (key_row > key_col).astype(BF16)

    chunks = [slice(c * ATTN_CHUNK, (c + 1) * ATTN_CHUNK) for c in range(n_chunks)]

    def past_mask(c):
        query = (c.start % tq) + lax.broadcasted_iota(jnp.int32, (ATTN_CHUNK, tk), 0)
        return lax.broadcasted_iota(jnp.int32, (ATTN_CHUNK, tk), 1) < query

    def visit(offsets, carry, first_is_diagonal):
        blocks = [[(k_ref[pl.ds(off, tk), pl.ds(p * LANES, LANES)],
                    v_ref[pl.ds(off, tk), pl.ds(p * LANES, LANES)]) for p in range(n_pairs)]
                  for off in offsets]
        pair = lambda c: c.start // (2 * tq)
        chains = [(b, c) for b in range(len(blocks)) for c in chunks]
        masked = lambda b: first_is_diagonal and b == 0
        z = [_dot_nt(q_stack[c], blocks[b][pair(c)][0]) for b, c in chains]
        log_keep, split = [], []
        for (b, c), zc in zip(chains, z):
            lk = -(jnp.maximum(zc, 0.0) + jnp.log(1.0 + jnp.exp(-jnp.abs(zc))))
            if masked(b):
                lk = jnp.where(past_mask(c), lk, 0.0)
            log_keep.append(lk)
            split.append(jnp.concatenate(_split_bf16(lk), axis=0))
        sums = [_dot(sp, later) for sp in split]
        carries = {c.start: (None if carry is None else carry[c]) for c in chunks}
        for i, (b, c) in enumerate(chains):
            after = sums[i][:ATTN_CHUNK] + sums[i][ATTN_CHUNK:]
            if carries[c.start] is not None:
                after = after + carries[c.start]
            a = jnp.exp(z[i] + log_keep[i] + after)
            if masked(b):
                a = jnp.where(past_mask(c), a, 0.0)
            acc_ref[c, :] += _dot(a.astype(BF16), blocks[b][pair(c)][1])
            total = jnp.sum(log_keep[i], axis=-1, keepdims=True)
            carries[c.start] = total if carries[c.start] is None else carries[c.start] + total
        return jnp.concatenate([carries[c.start] for c in chunks], axis=0)

    acc_ref[...] = jnp.zeros_like(acc_ref)
    start = pl.multiple_of(qi * tk, tk)

    @pl.when(qi == 0)
    def _():
        carry_ref[...] = visit([start], None, first_is_diagonal=True)

    @pl.when(qi > 0)
    def _():
        carry_ref[...] = visit([start, pl.multiple_of(start - tk, tk)], None, first_is_diagonal=True)

    def more(state):
        step, carry = state
        return (step < qi) & (jnp.max(carry) >= EXP_UNDERFLOW_F32)

    def body(state):
        step, carry = state
        off = pl.multiple_of((qi - 1 - step) * tk, tk)
        return step + 1, visit([off], carry, first_is_diagonal=False)

    lax.while_loop(more, body, (jnp.minimum(qi, 1), carry_ref[...]))

    for p in range(n_pairs):
        o = jnp.where(lane < SB_HEAD_DIM, acc_ref[pl.ds(2 * p * tq, tq), :],
                      acc_ref[pl.ds((2 * p + 1) * tq, tq), :])
        sq = o * o
        s0 = jnp.sum(jnp.where(lane < SB_HEAD_DIM, sq, 0.0), axis=-1, keepdims=True)
        s1 = jnp.sum(jnp.where(lane >= SB_HEAD_DIM, sq, 0.0), axis=-1, keepdims=True)
        ms = jnp.where(lane < SB_HEAD_DIM, s0, s1) * (1.0 / SB_HEAD_DIM)
        cols = pl.ds(p * LANES, LANES)
        o_ref[:, cols] = (o * lax.rsqrt(ms + NORM_EPS) * g_ref[:, cols]).astype(o_ref.dtype)


def _attention(q, k, v, sb_g):
    b, s, _ = q.shape
    t = min(ATTN_BLOCK, s)
    width = LANES * ATTN_PAIRS
    blk = pl.BlockSpec((None, t, width), lambda bi, p, i: (bi, i, p))
    seq = pl.BlockSpec((None, s, width), lambda bi, p, i: (bi, 0, p))
    return pl.pallas_call(
        _attention_kernel,
        grid=(b, SB_WIDTH // width, s // t),
        in_specs=[blk, seq, seq, pl.BlockSpec((1, width), lambda bi, p, i: (0, p))],
        out_specs=blk,
        out_shape=jax.ShapeDtypeStruct((b, s, SB_WIDTH), BF16),
        scratch_shapes=[pltpu.VMEM((2 * ATTN_PAIRS * t, LANES), F32),
                        pltpu.VMEM((2 * ATTN_PAIRS * t, 1), F32)],
        compiler_params=pltpu.CompilerParams(
            dimension_semantics=("parallel", "parallel", "arbitrary"),
            vmem_limit_bytes=VMEM_LIMIT_BYTES),
        name="attention",
    )(q, k, v, sb_g)


def _mix_route_kernel(x_ref, sb_ref, u_ref, halo_ref, wout_ref, wpool_ref, pscale_ref, g2_ref,
                      wr_ref, br_ref,
                      x1_ref, h2p_ref, idx_ref, gate_ref, rank_ref, count_ref,
                      count_acc, *, tiles_per_seq):
    tm = x_ref.shape[0]
    i = pl.program_id(0)
    seq_tile = i % tiles_per_seq

    t_main = lax.broadcasted_iota(jnp.int32, (tm, tm), 0)
    s_main = lax.broadcasted_iota(jnp.int32, (tm, tm), 1)
    t_halo = lax.broadcasted_iota(jnp.int32, (tm, POOL_HALO), 0)
    s_halo = lax.broadcasted_iota(jnp.int32, (tm, POOL_HALO), 1) - POOL_HALO
    pos = seq_tile * tm + lax.broadcasted_iota(jnp.int32, (tm, 1), 0)
    u = u_ref[...]
    halo = jnp.where(seq_tile > 0, halo_ref[...], jnp.zeros_like(halo_ref))
    pooled_groups = []
    for g, w in enumerate(POOL_WINDOWS):
        cols = pl.ds(g * POOL_GROUP_DIM, POOL_GROUP_DIM)
        band_main = ((s_main <= t_main) & (s_main > t_main - w)).astype(BF16)
        band_halo = (s_halo > t_halo - w).astype(BF16)
        p = u[:, g * POOL_GROUP_DIM:(g + 1) * POOL_GROUP_DIM]
        win = _dot(band_main, p) + _dot(band_halo, halo[:, g * POOL_GROUP_DIM:(g + 1) * POOL_GROUP_DIM])
        cnt = jnp.minimum(pos + 1, w).astype(F32)
        mix = win / cnt - p.astype(F32)
        pooled = _dot(mix.astype(BF16), wpool_ref[g]) * pscale_ref[:, cols]
        pooled_groups.append(pooled.astype(BF16))
    mixed = jnp.concatenate([sb_ref[...]] + pooled_groups, axis=1)
    y = _dot(mixed, wout_ref[...])

    x1 = x_ref[...] + y
    x1_ref[...] = x1
    ms = jnp.mean(x1 * x1, axis=-1, keepdims=True)
    h2 = x1 * lax.rsqrt(ms + NORM_EPS) * g2_ref[...]
    half = h2.shape[1] // 2
    h2p_ref[...] = _pack_bf16_pair(h2[:, :half], h2[:, half:])

    h_hi, h_lo = _split_bf16(h2)
    w_hi = wr_ref[0]
    w_lo = wr_ref[1]
    logits = _dot_nt(w_hi, h_hi) + _dot_nt(w_hi, h_lo) + _dot_nt(w_lo, h_hi) + br_ref[...]

    eidx = lax.broadcasted_iota(jnp.int32, (N_EXPERTS, tm), 0)
    work = logits
    vals, ids, hots = [], [], []
    for _ in range(TOP_K):
        m = jnp.max(work, axis=0, keepdims=True)
        sel = jnp.min(jnp.where(work == m, eidx, N_EXPERTS), axis=0, keepdims=True)
        hot = eidx == sel
        work = jnp.where(hot, -jnp.inf, work)
        vals.append(m)
        ids.append(sel)
        hots.append(hot)
    exps = [jnp.exp(v - vals[0]) for v in vals]
    denom = exps[0] + exps[1] + exps[2] + exps[3]

    @pl.when(i == 0)
    def _():
        count_acc[...] = jnp.zeros_like(count_acc)

    chosen = hots[0] | hots[1] | hots[2] | hots[3]
    earlier = (t_main < s_main).astype(BF16)
    before = _dot(chosen.astype(BF16), earlier) + count_acc[:, 0:1]
    for r in range(TOP_K):
        idx_ref[r:r + 1, :] = ids[r]
        gate_ref[r:r + 1, :] = exps[r] / denom
        rank = jnp.sum(jnp.where(hots[r], before, 0.0), axis=0, keepdims=True)
        rank_ref[r:r + 1, :] = rank.astype(jnp.int32)
    count_acc[...] += jnp.sum(chosen.astype(F32), axis=1, keepdims=True)
    count_ref[...] = count_acc[...]


def _mix_route(xf, sb, u, w_out, w_pool, pool_scale, g2, wr_split, b_router, seq_len):
    n, d = xf.shape
    tm = min(MIX_ROWS, seq_len)
    tiles_per_seq = seq_len // tm
    halo_blocks = tm // POOL_HALO
    row = lambda width: pl.BlockSpec((tm, width), lambda i: (i, 0))
    full = lambda shape: pl.BlockSpec(shape, lambda i: (0,) * len(shape))
    tok = pl.BlockSpec((TOP_K, tm), lambda i: (0, i))
    return pl.pallas_call(
        functools.partial(_mix_route_kernel, tiles_per_seq=tiles_per_seq),
        grid=(n // tm,),
        in_specs=[row(d), row(SB_WIDTH), row(POOL_WIDTH),
                  pl.BlockSpec((POOL_HALO, POOL_WIDTH), lambda i: (jnp.maximum(i * halo_blocks - 1, 0), 0)),
                  full(w_out.shape), full(w_pool.shape), full(pool_scale.shape), full(g2.shape),
                  full(wr_split.shape), full(b_router.shape)],
        out_specs=[row(d), row(d // 2), tok, tok, tok, full((N_EXPERTS, LANES))],
        out_shape=[jax.ShapeDtypeStruct((n, d), F32),
                   jax.ShapeDtypeStruct((n, d // 2), jnp.uint32),
                   jax.ShapeDtypeStruct((TOP_K, n), jnp.int32),
                   jax.ShapeDtypeStruct((TOP_K, n), F32),
                   jax.ShapeDtypeStruct((TOP_K, n), jnp.int32),
                   jax.ShapeDtypeStruct((N_EXPERTS, LANES), F32)],
        scratch_shapes=[pltpu.VMEM((N_EXPERTS, LANES), F32)],
        compiler_params=pltpu.CompilerParams(
            dimension_semantics=("arbitrary",), vmem_limit_bytes=VMEM_LIMIT_BYTES),
        name="mix_route",
    )(xf, sb, u, u, w_out, w_pool, pool_scale, g2, wr_split, b_router)


def _for_each_row_dma(tm, make_copy, action):
    def group(g, c):
        base = pl.multiple_of(g * ROW_DMA_GROUP, ROW_DMA_GROUP)
        for j in range(ROW_DMA_GROUP):
            for r in range(TOP_K):
                action(make_copy(base + j, r, (base + j) * TOP_K + r), r)
        return c

    lax.fori_loop(0, tm // ROW_DMA_GROUP, group, 0)


def _start_row_dma(copy, r):
    copy.start(priority=r % 2)


def _wait_row_dma(copy, r):
    copy.wait()


def _scatter_rows_kernel(dest_ref, h_ref, buf_in_ref, buf_ref, sem):
    del buf_in_ref
    tm = h_ref.shape[0]

    def copy(t, r, flat):
        return pltpu.make_async_copy(h_ref.at[pl.ds(t, 1), :],
                                     buf_ref.at[pl.ds(dest_ref[flat], 1), :], sem)

    _for_each_row_dma(tm, copy, _start_row_dma)
    _for_each_row_dma(tm, copy, _wait_row_dma)


def _scatter_rows(dest_flat, h2p, n_rows):
    n, width = h2p.shape
    tm = min(SCATTER_ROWS, n)
    zeros = jnp.zeros((n_rows, width), h2p.dtype)
    return pl.pallas_call(
        _scatter_rows_kernel,
        grid=(n // tm,),
        in_specs=[pl.BlockSpec((tm * TOP_K,), lambda i: (i,), memory_space=pltpu.SMEM),
                  pl.BlockSpec((tm, width), lambda i: (i, 0)),
                  pl.BlockSpec(memory_space=pl.ANY)],
        out_specs=pl.BlockSpec(memory_space=pl.ANY),
        out_shape=jax.ShapeDtypeStruct((n_rows, width), h2p.dtype),
        scratch_shapes=[pltpu.SemaphoreType.DMA(())],
        input_output_aliases={2: 0},
        compiler_params=pltpu.CompilerParams(
            dimension_semantics=("arbitrary",), vmem_limit_bytes=VMEM_LIMIT_BYTES),
        name="scatter_rows",
    )(dest_flat, h2p, zeros)


def _expert_ffn_kernel(block_e_ref, n_used_ref, x_ref, wgu_ref, bgu_ref, wd_ref, bd_ref, y_ref,
                       wgu_bf, wd_bf):
    i = pl.program_id(0)
    d_ff = wd_ref.shape[0]
    half = x_ref.shape[1]

    @pl.when(i < n_used_ref[0])
    def _():
        prev = block_e_ref[jnp.maximum(i - 1, 0)]

        @pl.when((i == 0) | (block_e_ref[i] != prev))
        def _():
            wgu_bf[...] = wgu_ref[...].astype(BF16)
            wd_bf[...] = wd_ref[...].astype(BF16)

        x_lo, x_hi = _unpack_bf16_pair(x_ref[...])
        gu = (_dot(x_lo.astype(BF16), wgu_bf[pl.ds(0, half), :])
              + _dot(x_hi.astype(BF16), wgu_bf[pl.ds(half, half), :]) + bgu_ref[...])
        gate = jnp.minimum(gu[:, :d_ff], SWIGLU_LIMIT)
        up = jnp.clip(gu[:, d_ff:], -SWIGLU_LIMIT, SWIGLU_LIMIT)
        act = (up + 1.0) * gate * (1.0 / (1.0 + jnp.exp(-SWIGLU_ALPHA * gate)))
        y = _dot(act.astype(BF16), wd_bf[...]) + bd_ref[...]
        y_ref[...] = _pack_bf16_pair(y[:, :half], y[:, half:])

    @pl.when(i >= n_used_ref[0])
    def _():
        y_ref[...] = jnp.zeros_like(y_ref)


def _expert_ffn(block_e, n_used, buf, w_gate_up, b_gate_up, w_down, b_down):
    n_rows, half = buf.shape
    _, d, two_f = w_gate_up.shape
    d_ff = w_down.shape[1]
    n_blocks = n_rows // FFN_ROWS

    def rows(i, be, nu):
        return (jnp.minimum(i, nu[0] - 1), 0)

    def expert(i, be, nu):
        return (be[jnp.minimum(i, nu[0] - 1)], 0, 0)

    grid_spec = pltpu.PrefetchScalarGridSpec(
        num_scalar_prefetch=2,
        grid=(n_blocks,),
        in_specs=[pl.BlockSpec((FFN_ROWS, half), rows),
                  pl.BlockSpec((None, d, two_f), expert),
                  pl.BlockSpec((None, 1, two_f), expert),
                  pl.BlockSpec((None, d_ff, d), expert),
                  pl.BlockSpec((None, 1, d), expert)],
        out_specs=pl.BlockSpec((FFN_ROWS, half), lambda i, be, nu: (i, 0)),
        scratch_shapes=[pltpu.VMEM((d, two_f), BF16), pltpu.VMEM((d_ff, d), BF16)],
    )
    return pl.pallas_call(
        _expert_ffn_kernel,
        grid_spec=grid_spec,
        out_shape=jax.ShapeDtypeStruct((n_rows, half), jnp.uint32),
        compiler_params=pltpu.CompilerParams(
            dimension_semantics=("arbitrary",), vmem_limit_bytes=VMEM_LIMIT_BYTES),
        name="expert_ffn",
    )(block_e, n_used, buf, w_gate_up, b_gate_up, w_down, b_down)


def _combine_kernel(dest_ref, dest_next_ref, x1_ref, gate_ref, g_ref, y_ref, o_ref, rows_ref, sem):
    tm = x1_ref.shape[0]
    i = pl.program_id(0)
    slot = i % 2

    def gather(table_ref, into):
        def copy(t, r, flat):
            return pltpu.make_async_copy(y_ref.at[pl.ds(table_ref[flat], 1), :],
                                         rows_ref.at[into, r, pl.ds(t, 1), :], sem.at[into])
        return copy

    @pl.when(i == 0)
    def _():
        _for_each_row_dma(tm, gather(dest_ref, 0), _start_row_dma)

    @pl.when(i + 1 < pl.num_programs(0))
    def _():
        _for_each_row_dma(tm, gather(dest_next_ref, 1 - slot), _start_row_dma)

    _for_each_row_dma(tm, gather(dest_ref, slot), _wait_row_dma)

    gates = gate_ref[...]
    x1 = x1_ref[...]
    half = rows_ref.shape[3]
    lo_sum = x1[:, :half]
    hi_sum = x1[:, half:]
    for r in range(TOP_K):
        lo, hi = _unpack_bf16_pair(rows_ref[slot, r])
        g = gates[:, r:r + 1]
        lo_sum += g * lo
        hi_sum += g * hi
    ms = (jnp.sum(lo_sum * lo_sum, axis=-1, keepdims=True)
          + jnp.sum(hi_sum * hi_sum, axis=-1, keepdims=True)) * (1.0 / (2 * half))
    inv = lax.rsqrt(ms + NORM_EPS)
    o_ref[:, :half] = lo_sum * inv * g_ref[:, :half]
    o_ref[:, half:] = hi_sum * inv * g_ref[:, half:]


def _combine(dest_flat, x1, gates_t, final_g, y):
    n, d = x1.shape
    tm = min(COMBINE_ROWS, n)
    steps = n // tm
    table = lambda index_map: pl.BlockSpec((tm * TOP_K,), index_map, memory_space=pltpu.SMEM)
    return pl.pallas_call(
        _combine_kernel,
        grid=(steps,),
        in_specs=[table(lambda i: (i,)),
                  table(lambda i: (jnp.minimum(i + 1, steps - 1),)),
                  pl.BlockSpec((tm, d), lambda i: (i, 0)),
                  pl.BlockSpec((tm, TOP_K), lambda i: (i, 0)),
                  pl.BlockSpec((1, d), lambda i: (0, 0)),
                  pl.BlockSpec(memory_space=pl.ANY)],
        out_specs=pl.BlockSpec((tm, d), lambda i: (i, 0)),
        out_shape=jax.ShapeDtypeStruct((n, d), F32),
        scratch_shapes=[pltpu.VMEM((2, TOP_K, tm, d // 2), jnp.uint32),
                        pltpu.SemaphoreType.DMA((2,))],
        compiler_params=pltpu.CompilerParams(
            dimension_semantics=("arbitrary",), vmem_limit_bytes=VMEM_LIMIT_BYTES),
        name="combine",
    )(dest_flat, dest_flat, x1, gates_t, final_g, y)


def _layer(x, norm1_g, w_in, sb_norm_g, w_pool, pool_scale, w_out, norm2_g,
           w_router, b_router, w_gate_up, b_gate_up, w_down, b_down, out_g):
    b, s, d = x.shape
    n = b * s
    xf = x.reshape(n, d)

    q, k, v, u = _in_proj(xf, norm1_g.reshape(1, d), w_in.astype(BF16))
    sb = _attention(q.reshape(b, s, SB_WIDTH), k.reshape(b, s, SB_WIDTH), v.reshape(b, s, SB_WIDTH),
                    sb_norm_g.reshape(1, SB_WIDTH))

    wr_t = w_router.T
    wr_hi = wr_t.astype(BF16)
    wr_lo = (wr_t - wr_hi.astype(F32)).astype(BF16)
    x1, h2p, idx, gates, rank, counts = _mix_route(
        xf, sb.reshape(n, SB_WIDTH), u, w_out.astype(BF16), w_pool.astype(BF16),
        pool_scale.reshape(1, POOL_WIDTH), norm2_g.reshape(1, d),
        jnp.stack([wr_hi, wr_lo]), b_router.reshape(N_EXPERTS, 1), s)

    counts = counts[:, 0].astype(jnp.int32)
    padded = (counts + FFN_ROWS - 1) // FFN_ROWS * FFN_ROWS
    pend = jnp.cumsum(padded)
    pstart = pend - padded
    experts = jnp.arange(N_EXPERTS, dtype=jnp.int32)[:, None, None]
    dest = rank + jnp.sum(jnp.where(idx[None] == experts, pstart[:, None, None], 0), axis=0)
    n_blocks = -(-(n * TOP_K) // FFN_ROWS) + N_EXPERTS
    n_used = (pend[-1] // FFN_ROWS).astype(jnp.int32).reshape(1)
    block_start = jnp.arange(n_blocks, dtype=jnp.int32) * FFN_ROWS
    block_e = jnp.minimum(jnp.sum(pend[None, :] <= block_start[:, None], axis=1),
                          N_EXPERTS - 1).astype(jnp.int32)

    dest_flat = dest.T.reshape(-1)
    buf = _scatter_rows(dest_flat, h2p, n_blocks * FFN_ROWS)
    y = _expert_ffn(block_e, n_used, buf, w_gate_up, b_gate_up.reshape(N_EXPERTS, 1, -1),
                    w_down, b_down.reshape(N_EXPERTS, 1, -1))
    out = _combine(dest_flat, x1, gates.T, out_g.reshape(1, d), y)
    return out.reshape(b, s, d)


def kernel(x, norm1_g, w_in, sb_norm_g, w_pool, pool_scale, w_out, norm2_g, w_router, b_router,
           w_gate_up, b_gate_up, w_down, b_down, final_norm_g):
    assert norm1_g.shape[0] == 1, "single-layer block"
    return _layer(x, norm1_g[0], w_in[0], sb_norm_g[0], w_pool[0], pool_scale[0], w_out[0],
                  norm2_g[0], w_router[0], b_router[0], w_gate_up[0], b_gate_up[0], w_down[0],
                  b_down[0], final_norm_g)
```

```python
import functools
import math

import jax
import jax.numpy as jnp
from jax import lax
from jax.experimental import pallas as pl
from jax.experimental.pallas import tpu as pltpu

NORM_EPS = 1e-5
SB_HEADS = 8
SB_HEAD_DIM = 64
SB_WIDTH = SB_HEADS * SB_HEAD_DIM
POOL_WINDOWS = (2, 4, 8, 16)
POOL_GROUP_DIM = 128
POOL_WIDTH = len(POOL_WINDOWS) * POOL_GROUP_DIM
POOL_HALO = 16
N_EXPERTS = 32
TOP_K = 4
SWIGLU_LIMIT = 7.0
SWIGLU_ALPHA = 1.702
EXP_UNDERFLOW_F32 = -105.0

LANES = 128
VMEM_LIMIT_BYTES = 56 * 1024 * 1024

PROJ_ROWS = 512
ATTN_BLOCK = 256
ATTN_CHUNK = 128
ATTN_PAIRS = 2
MIX_ROWS = 512
ROW_DMA_GROUP = 8
SCATTER_ROWS = 256
FFN_ROWS = 512
COMBINE_ROWS = 256

F32 = jnp.float32
BF16 = jnp.bfloat16


def _dot(a, b):
    return jnp.dot(a, b, preferred_element_type=F32)


def _dot_nt(a, b):
    return lax.dot_general(a, b, (((1,), (1,)), ((), ())), preferred_element_type=F32)


def _split_bf16(x):
    hi = x.astype(BF16)
    lo = (x - hi.astype(F32)).astype(BF16)
    return hi, lo


def _pack_bf16_pair(lo_f32, hi_f32):
    lo_bits = pltpu.bitcast(lo_f32.astype(BF16).astype(F32), jnp.uint32)
    hi_bits = pltpu.bitcast(hi_f32.astype(BF16).astype(F32), jnp.uint32)
    return (hi_bits & jnp.uint32(0xFFFF0000)) | (lo_bits >> 16)


def _unpack_bf16_pair(word):
    lo = pltpu.bitcast(word << 16, F32)
    hi = pltpu.bitcast(word & jnp.uint32(0xFFFF0000), F32)
    return lo, hi


def _in_proj_kernel(x_ref, g_ref, w_ref, q_ref, k_ref, v_ref, u_ref):
    x = x_ref[...]
    ms = jnp.mean(x * x, axis=-1, keepdims=True)
    h = (x * lax.rsqrt(ms + NORM_EPS) * g_ref[...]).astype(BF16)
    proj = _dot(h, w_ref[...])
    scale = 1.0 / math.sqrt(SB_HEAD_DIM)
    q_ref[...] = (proj[:, :SB_WIDTH] * scale).astype(BF16)
    k_ref[...] = proj[:, SB_WIDTH:2 * SB_WIDTH].astype(BF16)
    v_ref[...] = proj[:, 2 * SB_WIDTH:3 * SB_WIDTH].astype(BF16)
    u_ref[...] = proj[:, 3 * SB_WIDTH:].astype(BF16)


def _in_proj(xf, g, w_bf16):
    n, d = xf.shape
    tm = min(PROJ_ROWS, n)
    wide = w_bf16.shape[1]
    out = jax.ShapeDtypeStruct((n, SB_WIDTH), BF16)
    row_spec = pl.BlockSpec((tm, SB_WIDTH), lambda i: (i, 0))
    return pl.pallas_call(
        _in_proj_kernel,
        grid=(n // tm,),
        in_specs=[pl.BlockSpec((tm, d), lambda i: (i, 0)),
                  pl.BlockSpec((1, d), lambda i: (0, 0)),
                  pl.BlockSpec((d, wide), lambda i: (0, 0))],
        out_specs=[row_spec, row_spec, row_spec, row_spec],
        out_shape=[out, out, out, jax.ShapeDtypeStruct((n, POOL_WIDTH), BF16)],
        compiler_params=pltpu.CompilerParams(
            dimension_semantics=("parallel",), vmem_limit_bytes=VMEM_LIMIT_BYTES),
        name="in_proj",
    )(xf, g, w_bf16)


def _attention_kernel(q_ref, k_ref, v_ref, g_ref, o_ref, acc_ref, carry_ref):
    tq = q_ref.shape[0]
    tk = tq
    n_pairs = q_ref.shape[1] // LANES
    rows = 2 * n_pairs * tq
    n_chunks = rows // ATTN_CHUNK
    qi = pl.program_id(2)
    lane = lax.broadcasted_iota(jnp.int32, (tq, LANES), 1)
    stacked = []
    for p in range(n_pairs):
        q = q_ref[:, pl.ds(p * LANES, LANES)].astype(F32)
        stacked += [jnp.where(lane < SB_HEAD_DIM, q, 0.0), jnp.where(lane >= SB_HEAD_DIM, q, 0.0)]
    q_stack = jnp.concatenate(stacked, axis=0).astype(BF16)

    key_row = lax.broadcasted_iota(jnp.int32, (tk, tk), 0)
    key_col = lax.broadcasted_iota(jnp.int32, (tk, tk), 1)
    later = (key_row > key_col).astype(BF16)

    chunks = [slice(c * ATTN_CHUNK, (c + 1) * ATTN_CHUNK) for c in range(n_chunks)]

    def past_mask(c):
        query = (c.start % tq) + lax.broadcasted_iota(jnp.int32, (ATTN_CHUNK, tk), 0)
        return lax.broadcasted_iota(jnp.int32, (ATTN_CHUNK, tk), 1) < query

    def visit(offsets, carry, first_is_diagonal):
        blocks = [[(k_ref[pl.ds(off, tk), pl.ds(p * LANES, LANES)],
                    v_ref[pl.ds(off, tk), pl.ds(p * LANES, LANES)]) for p in range(n_pairs)]
                  for off in offsets]
        pair = lambda c: c.start // (2 * tq)
        chains = [(b, c) for b in range(len(blocks)) for c in chunks]
        masked = lambda b: first_is_diagonal and b == 0
        z = [_dot_nt(q_stack[c], blocks[b][pair(c)][0]) for b, c in chains]
        log_keep, split = [], []
        for (b, c), zc in zip(chains, z):
            lk = -(jnp.maximum(zc, 0.0) + jnp.log(1.0 + jnp.exp(-jnp.abs(zc))))
            if masked(b):
                lk = jnp.where(past_mask(c), lk, 0.0)
            log_keep.append(lk)
            split.append(jnp.concatenate(_split_bf16(lk), axis=0))
        sums = [_dot(sp, later) for sp in split]
        carries = {c.start: (None if carry is None else carry[c]) for c in chunks}
        for i, (b, c) in enumerate(chains):
            after = sums[i][:ATTN_CHUNK] + sums[i][ATTN_CHUNK:]
            if carries[c.start] is not None:
                after = after + carries[c.start]
            a = jnp.exp(z[i] + log_keep[i] + after)
            if masked(b):
                a = jnp.where(past_mask(c), a, 0.0)
            acc_ref[c, :] += _dot(a.astype(BF16), blocks[b][pair(c)][1])
            total = jnp.sum(log_keep[i], axis=-1, keepdims=True)
            carries[c.start] = total if carries[c.start] is None else carries[c.start] + total
        return jnp.concatenate([carries[c.start] for c in chunks], axis=0)

    acc_ref[...] = jnp.zeros_like(acc_ref)
    start = pl.multiple_of(qi * tk, tk)

    @pl.when(qi == 0)
    def _():
        carry_ref[...] = visit([start], None, first_is_diagonal=True)

    @pl.when(qi > 0)
    def _():
        carry_ref[...] = visit([start, pl.multiple_of(start - tk, tk)], None, first_is_diagonal=True)

    def more(state):
        step, carry = state
        return (step < qi) & (jnp.max(carry) >= EXP_UNDERFLOW_F32)

    def body(state):
        step, carry = state
        off = pl.multiple_of((qi - 1 - step) * tk, tk)
        return step + 1, visit([off], carry, first_is_diagonal=False)

    lax.while_loop(more, body, (jnp.minimum(qi, 1), carry_ref[...]))

    for p in range(n_pairs):
        o = jnp.where(lane < SB_HEAD_DIM, acc_ref[pl.ds(2 * p * tq, tq), :],
                      acc_ref[pl.ds((2 * p + 1) * tq, tq), :])
        sq = o * o
        s0 = jnp.sum(jnp.where(lane < SB_HEAD_DIM, sq, 0.0), axis=-1, keepdims=True)
        s1 = jnp.sum(jnp.where(lane >= SB_HEAD_DIM, sq, 0.0), axis=-1, keepdims=True)
        ms = jnp.where(lane < SB_HEAD_DIM, s0, s1) * (1.0 / SB_HEAD_DIM)
        cols = pl.ds(p * LANES, LANES)
        o_ref[:, cols] = (o * lax.rsqrt(ms + NORM_EPS) * g_ref[:, cols]).astype(o_ref.dtype)


def _attention(q, k, v, sb_g):
    b, s, _ = q.shape
    t = min(ATTN_BLOCK, s)
    width = LANES * ATTN_PAIRS
    blk = pl.BlockSpec((None, t, width), lambda bi, p, i: (bi, i, p))
    seq = pl.BlockSpec((None, s, width), lambda bi, p, i: (bi, 0, p))
    return pl.pallas_call(
        _attention_kernel,
        grid=(b, SB_WIDTH // width, s // t),
        in_specs=[blk, seq, seq, pl.BlockSpec((1, width), lambda bi, p, i: (0, p))],
        out_specs=blk,
        out_shape=jax.ShapeDtypeStruct((b, s, SB_WIDTH), BF16),
        scratch_shapes=[pltpu.VMEM((2 * ATTN_PAIRS * t, LANES), F32),
                        pltpu.VMEM((2 * ATTN_PAIRS * t, 1), F32)],
        compiler_params=pltpu.CompilerParams(
            dimension_semantics=("parallel", "parallel", "arbitrary"),
            vmem_limit_bytes=VMEM_LIMIT_BYTES),
        name="attention",
    )(q, k, v, sb_g)


def _mix_route_kernel(x_ref, sb_ref, u_ref, halo_ref, wout_ref, wpool_ref, pscale_ref, g2_ref,
                      wr_ref, br_ref,
                      x1_ref, h2p_ref, idx_ref, gate_ref, rank_ref, count_ref,
                      count_acc, *, tiles_per_seq):
    tm = x_ref.shape[0]
    i = pl.program_id(0)
    seq_tile = i % tiles_per_seq

    t_main = lax.broadcasted_iota(jnp.int32, (tm, tm), 0)
    s_main = lax.broadcasted_iota(jnp.int32, (tm, tm), 1)
    t_halo = lax.broadcasted_iota(jnp.int32, (tm, POOL_HALO), 0)
    s_halo = lax.broadcasted_iota(jnp.int32, (tm, POOL_HALO), 1) - POOL_HALO
    pos = seq_tile * tm + lax.broadcasted_iota(jnp.int32, (tm, 1), 0)
    u = u_ref[...]
    halo = jnp.where(seq_tile > 0, halo_ref[...], jnp.zeros_like(halo_ref))
    pooled_groups = []
    for g, w in enumerate(POOL_WINDOWS):
        cols = pl.ds(g * POOL_GROUP_DIM, POOL_GROUP_DIM)
        band_main = ((s_main <= t_main) & (s_main > t_main - w)).astype(BF16)
        band_halo = (s_halo > t_halo - w).astype(BF16)
        p = u[:, g * POOL_GROUP_DIM:(g + 1) * POOL_GROUP_DIM]
        win = _dot(band_main, p) + _dot(band_halo, halo[:, g * POOL_GROUP_DIM:(g + 1) * POOL_GROUP_DIM])
        cnt = jnp.minimum(pos + 1, w).astype(F32)
        mix = win / cnt - p.astype(F32)
        pooled = _dot(mix.astype(BF16), wpool_ref[g]) * pscale_ref[:, cols]
        pooled_groups.append(pooled.astype(BF16))
    mixed = jnp.concatenate([sb_ref[...]] + pooled_groups, axis=1)
    y = _dot(mixed, wout_ref[...])

    x1 = x_ref[...] + y
    x1_ref[...] = x1
    ms = jnp.mean(x1 * x1, axis=-1, keepdims=True)
    h2 = x1 * lax.rsqrt(ms + NORM_EPS) * g2_ref[...]
    half = h2.shape[1] // 2
    h2p_ref[...] = _pack_bf16_pair(h2[:, :half], h2[:, half:])

    h_hi, h_lo = _split_bf16(h2)
    w_hi = wr_ref[0]
    w_lo = wr_ref[1]
    logits = _dot_nt(w_hi, h_hi) + _dot_nt(w_hi, h_lo) + _dot_nt(w_lo, h_hi) + br_ref[...]

    eidx = lax.broadcasted_iota(jnp.int32, (N_EXPERTS, tm), 0)
    work = logits
    vals, ids, hots = [], [], []
    for _ in range(TOP_K):
        m = jnp.max(work, axis=0, keepdims=True)
        sel = jnp.min(jnp.where(work == m, eidx, N_EXPERTS), axis=0, keepdims=True)
        hot = eidx == sel
        work = jnp.where(hot, -jnp.inf, work)
        vals.append(m)
        ids.append(sel)
        hots.append(hot)
    exps = [jnp.exp(v - vals[0]) for v in vals]
    denom = exps[0] + exps[1] + exps[2] + exps[3]

    @pl.when(i == 0)
    def _():
        count_acc[...] = jnp.zeros_like(count_acc)

    chosen = hots[0] | hots[1] | hots[2] | hots[3]
    earlier = (t_main < s_main).astype(BF16)
    before = _dot(chosen.astype(BF16), earlier) + count_acc[:, 0:1]
    for r in range(TOP_K):
        idx_ref[r:r + 1, :] = ids[r]
        gate_ref[r:r + 1, :] = exps[r] / denom
        rank = jnp.sum(jnp.where(hots[r], before, 0.0), axis=0, keepdims=True)
        rank_ref[r:r + 1, :] = rank.astype(jnp.int32)
    count_acc[...] += jnp.sum(chosen.astype(F32), axis=1, keepdims=True)
    count_ref[...] = count_acc[...]


def _mix_route(xf, sb, u, w_out, w_pool, pool_scale, g2, wr_split, b_router, seq_len):
    n, d = xf.shape
    tm = min(MIX_ROWS, seq_len)
    tiles_per_seq = seq_len // tm
    halo_blocks = tm // POOL_HALO
    row = lambda width: pl.BlockSpec((tm, width), lambda i: (i, 0))
    full = lambda shape: pl.BlockSpec(shape, lambda i: (0,) * len(shape))
    tok = pl.BlockSpec((TOP_K, tm), lambda i: (0, i))
    return pl.pallas_call(
        functools.partial(_mix_route_kernel, tiles_per_seq=tiles_per_seq),
        grid=(n // tm,),
        in_specs=[row(d), row(SB_WIDTH), row(POOL_WIDTH),
                  pl.BlockSpec((POOL_HALO, POOL_WIDTH), lambda i: (jnp.maximum(i * halo_blocks - 1, 0), 0)),
                  full(w_out.shape), full(w_pool.shape), full(pool_scale.shape), full(g2.shape),
                  full(wr_split.shape), full(b_router.shape)],
        out_specs=[row(d), row(d // 2), tok, tok, tok, full((N_EXPERTS, LANES))],
        out_shape=[jax.ShapeDtypeStruct((n, d), F32),
                   jax.ShapeDtypeStruct((n, d // 2), jnp.uint32),
                   jax.ShapeDtypeStruct((TOP_K, n), jnp.int32),
                   jax.ShapeDtypeStruct((TOP_K, n), F32),
                   jax.ShapeDtypeStruct((TOP_K, n), jnp.int32),
                   jax.ShapeDtypeStruct((N_EXPERTS, LANES), F32)],
        scratch_shapes=[pltpu.VMEM((N_EXPERTS, LANES), F32)],
        compiler_params=pltpu.CompilerParams(
            dimension_semantics=("arbitrary",), vmem_limit_bytes=VMEM_LIMIT_BYTES),
        name="mix_route",
    )(xf, sb, u, u, w_out, w_pool, pool_scale, g2, wr_split, b_router)


def _for_each_row_dma(tm, make_copy, action):
    def group(g, c):
        for j in range(ROW_DMA_GROUP):
            for r in range(TOP_K):
                action(make_copy(g, j, r, (g * ROW_DMA_GROUP + j) * TOP_K + r), r)
        return c

    lax.fori_loop(0, tm // ROW_DMA_GROUP, group, 0)


def _start_row_dma(copy, r):
    copy.start(priority=r % 2)


def _wait_row_dma(copy, r):
    copy.wait()


def _scatter_rows_kernel(dest_ref, h_ref, buf_in_ref, buf_ref, sem):
    del buf_in_ref
    tm = h_ref.shape[0] * ROW_DMA_GROUP

    def copy(g, j, r, flat):
        return pltpu.make_async_copy(h_ref.at[g, pl.ds(j, 1), :],
                                     buf_ref.at[pl.ds(dest_ref[flat], 1), :], sem)

    _for_each_row_dma(tm, copy, _start_row_dma)
    _for_each_row_dma(tm, copy, _wait_row_dma)


def _scatter_rows(dest_flat, h2p, n_rows):
    n, width = h2p.shape
    tm = min(SCATTER_ROWS, n)
    zeros = jnp.zeros((n_rows, width), h2p.dtype)
    return pl.pallas_call(
        _scatter_rows_kernel,
        grid=(n // tm,),
        in_specs=[pl.BlockSpec((tm * TOP_K,), lambda i: (i,), memory_space=pltpu.SMEM),
                  pl.BlockSpec((tm // ROW_DMA_GROUP, ROW_DMA_GROUP, width), lambda i: (i, 0, 0)),
                  pl.BlockSpec(memory_space=pl.ANY)],
        out_specs=pl.BlockSpec(memory_space=pl.ANY),
        out_shape=jax.ShapeDtypeStruct((n_rows, width), h2p.dtype),
        scratch_shapes=[pltpu.SemaphoreType.DMA(())],
        input_output_aliases={2: 0},
        compiler_params=pltpu.CompilerParams(
            dimension_semantics=("arbitrary",), vmem_limit_bytes=VMEM_LIMIT_BYTES),
        name="scatter_rows",
    )(dest_flat, h2p.reshape(n // ROW_DMA_GROUP, ROW_DMA_GROUP, width), zeros)


def _expert_ffn_kernel(block_e_ref, n_used_ref, x_ref, wgu_ref, bgu_ref, wd_ref, bd_ref, y_ref,
                       wgu_bf, wd_bf):
    i = pl.program_id(0)
    d_ff = wd_ref.shape[0]
    half = x_ref.shape[1]

    @pl.when(i < n_used_ref[0])
    def _():
        prev = block_e_ref[jnp.maximum(i - 1, 0)]

        @pl.when((i == 0) | (block_e_ref[i] != prev))
        def _():
            wgu_bf[...] = wgu_ref[...].astype(BF16)
            wd_bf[...] = wd_ref[...].astype(BF16)

        x_lo, x_hi = _unpack_bf16_pair(x_ref[...])
        gu = (_dot(x_lo.astype(BF16), wgu_bf[pl.ds(0, half), :])
              + _dot(x_hi.astype(BF16), wgu_bf[pl.ds(half, half), :]) + bgu_ref[...])
        gate = jnp.minimum(gu[:, :d_ff], SWIGLU_LIMIT)
        up = jnp.clip(gu[:, d_ff:], -SWIGLU_LIMIT, SWIGLU_LIMIT)
        act = (up + 1.0) * gate * (1.0 / (1.0 + jnp.exp(-SWIGLU_ALPHA * gate)))
        y = _dot(act.astype(BF16), wd_bf[...]) + bd_ref[...]
        y_ref[...] = _pack_bf16_pair(y[:, :half], y[:, half:])

    @pl.when(i >= n_used_ref[0])
    def _():
        y_ref[...] = jnp.zeros_like(y_ref)


def _expert_ffn(block_e, n_used, buf, w_gate_up, b_gate_up, w_down, b_down):
    n_rows, half = buf.shape
    _, d, two_f = w_gate_up.shape
    d_ff = w_down.shape[1]
    n_blocks = n_rows // FFN_ROWS

    def rows(i, be, nu):
        return (jnp.minimum(i, nu[0] - 1), 0)

    def expert(i, be, nu):
        return (be[jnp.minimum(i, nu[0] - 1)], 0, 0)

    grid_spec = pltpu.PrefetchScalarGridSpec(
        num_scalar_prefetch=2,
        grid=(n_blocks,),
        in_specs=[pl.BlockSpec((FFN_ROWS, half), rows),
                  pl.BlockSpec((None, d, two_f), expert),
                  pl.BlockSpec((None, 1, two_f), expert),
                  pl.BlockSpec((None, d_ff, d), expert),
                  pl.BlockSpec((None, 1, d), expert)],
        out_specs=pl.BlockSpec((FFN_ROWS, half), lambda i, be, nu: (i, 0)),
        scratch_shapes=[pltpu.VMEM((d, two_f), BF16), pltpu.VMEM((d_ff, d), BF16)],
    )
    return pl.pallas_call(
        _expert_ffn_kernel,
        grid_spec=grid_spec,
        out_shape=jax.ShapeDtypeStruct((n_rows, half), jnp.uint32),
        compiler_params=pltpu.CompilerParams(
            dimension_semantics=("arbitrary",), vmem_limit_bytes=VMEM_LIMIT_BYTES),
        name="expert_ffn",
    )(block_e, n_used, buf, w_gate_up, b_gate_up, w_down, b_down)


def _combine_kernel(dest_ref, dest_next_ref, x1_ref, gate_ref, g_ref, y_ref, o_ref, rows_ref, sem):
    tm = rows_ref.shape[2] * ROW_DMA_GROUP
    half = rows_ref.shape[4]
    step = pl.program_id(0)

    def gather(table_ref, table_offset, slot):
        def copy(g, j, r, flat):
            return pltpu.make_async_copy(y_ref.at[pl.ds(table_ref[table_offset + flat], 1), :],
                                         rows_ref.at[slot, r, g, pl.ds(j, 1), :], sem.at[slot])
        return copy

    def finish(slot):
        _for_each_row_dma(tm, gather(dest_ref, slot * tm * TOP_K, slot), _wait_row_dma)
        rows = pl.ds(slot * tm, tm)
        gates = gate_ref[rows, :]
        lo_sum = x1_ref[rows, pl.ds(0, half)]
        hi_sum = x1_ref[rows, pl.ds(half, half)]
        for r in range(TOP_K):
            lo, hi = _unpack_bf16_pair(rows_ref[slot, r].reshape(tm, half))
            g = gates[:, r:r + 1]
            lo_sum += g * lo
            hi_sum += g * hi
        ms = (jnp.sum(lo_sum * lo_sum, axis=-1, keepdims=True)
              + jnp.sum(hi_sum * hi_sum, axis=-1, keepdims=True)) * (1.0 / (2 * half))
        inv = lax.rsqrt(ms + NORM_EPS)
        o_ref[rows, pl.ds(0, half)] = lo_sum * inv * g_ref[:, pl.ds(0, half)]
        o_ref[rows, pl.ds(half, half)] = hi_sum * inv * g_ref[:, pl.ds(half, half)]

    @pl.when(step == 0)
    def _():
        _for_each_row_dma(tm, gather(dest_ref, 0, 0), _start_row_dma)

    _for_each_row_dma(tm, gather(dest_ref, tm * TOP_K, 1), _start_row_dma)
    finish(0)

    @pl.when(step + 1 < pl.num_programs(0))
    def _():
        _for_each_row_dma(tm, gather(dest_next_ref, 0, 0), _start_row_dma)

    finish(1)


def _combine(dest_flat, x1, gates_t, final_g, y):
    n, d = x1.shape
    tm = min(COMBINE_ROWS, n // 2)
    steps = n // (2 * tm)
    table = lambda index_map: pl.BlockSpec((2 * tm * TOP_K,), index_map, memory_space=pltpu.SMEM)
    return pl.pallas_call(
        _combine_kernel,
        grid=(steps,),
        in_specs=[table(lambda i: (i,)),
                  table(lambda i: (jnp.minimum(i + 1, steps - 1),)),
                  pl.BlockSpec((2 * tm, d), lambda i: (i, 0)),
                  pl.BlockSpec((2 * tm, TOP_K), lambda i: (i, 0)),
                  pl.BlockSpec((1, d), lambda i: (0, 0)),
                  pl.BlockSpec(memory_space=pl.ANY)],
        out_specs=pl.BlockSpec((2 * tm, d), lambda i: (i, 0)),
        out_shape=jax.ShapeDtypeStruct((n, d), F32),
        scratch_shapes=[pltpu.VMEM((2, TOP_K, tm // ROW_DMA_GROUP, ROW_DMA_GROUP, d // 2), jnp.uint32),
                        pltpu.SemaphoreType.DMA((2,))],
        compiler_params=pltpu.CompilerParams(
            dimension_semantics=("arbitrary",), vmem_limit_bytes=VMEM_LIMIT_BYTES),
        name="combine",
    )(dest_flat, dest_flat, x1, gates_t, final_g, y)


def _layer(x, norm1_g, w_in, sb_norm_g, w_pool, pool_scale, w_out, norm2_g,
           w_router, b_router, w_gate_up, b_gate_up, w_down, b_down, out_g):
    b, s, d = x.shape
    n = b * s
    xf = x.reshape(n, d)

    q, k, v, u = _in_proj(xf, norm1_g.reshape(1, d), w_in.astype(BF16))
    sb = _attention(q.reshape(b, s, SB_WIDTH), k.reshape(b, s, SB_WIDTH), v.reshape(b, s, SB_WIDTH),
                    sb_norm_g.reshape(1, SB_WIDTH))

    wr_t = w_router.T
    wr_hi = wr_t.astype(BF16)
    wr_lo = (wr_t - wr_hi.astype(F32)).astype(BF16)
    x1, h2p, idx, gates, rank, counts = _mix_route(
        xf, sb.reshape(n, SB_WIDTH), u, w_out.astype(BF16), w_pool.astype(BF16),
        pool_scale.reshape(1, POOL_WIDTH), norm2_g.reshape(1, d),
        jnp.stack([wr_hi, wr_lo]), b_router.reshape(N_EXPERTS, 1), s)

    counts = counts[:, 0].astype(jnp.int32)
    padded = (counts + FFN_ROWS - 1) // FFN_ROWS * FFN_ROWS
    pend = jnp.cumsum(padded)
    pstart = pend - padded
    experts = jnp.arange(N_EXPERTS, dtype=jnp.int32)[:, None, None]
    dest = rank + jnp.sum(jnp.where(idx[None] == experts, pstart[:, None, None], 0), axis=0)
    n_blocks = -(-(n * TOP_K) // FFN_ROWS) + N_EXPERTS
    n_used = (pend[-1] // FFN_ROWS).astype(jnp.int32).reshape(1)
    block_start = jnp.arange(n_blocks, dtype=jnp.int32) * FFN_ROWS
    block_e = jnp.minimum(jnp.sum(pend[None, :] <= block_start[:, None], axis=1),
                          N_EXPERTS - 1).astype(jnp.int32)

    dest_flat = dest.T.reshape(-1)
    buf = _scatter_rows(dest_flat, h2p, n_blocks * FFN_ROWS)
    y = _expert_ffn(block_e, n_used, buf, w_gate_up, b_gate_up.reshape(N_EXPERTS, 1, -1),
                    w_down, b_down.reshape(N_EXPERTS, 1, -1))
    out = _combine(dest_flat, x1, gates.T, out_g.reshape(1, d), y)
    return out.reshape(b, s, d)


def kernel(x, norm1_g, w_in, sb_norm_g, w_pool, pool_scale, w_out, norm2_g, w_router, b_router,
           w_gate_up, b_gate_up, w_down, b_down, final_norm_g):
    assert norm1_g.shape[0] == 1, "single-layer block"
    return _layer(x, norm1_g[0], w_in[0], sb_norm_g[0], w_pool[0], pool_scale[0], w_out[0],
                  norm2_g[0], w_router[0], b_router[0], w_gate_up[0], b_gate_up[0], w_down[0],
                  b_down[0], final_norm_g)
```

```python
import functools
import math

import jax
import jax.numpy as jnp
from jax import lax
from jax.experimental import pallas as pl
from jax.experimental.pallas import tpu as pltpu

NORM_EPS = 1e-5
SB_HEADS = 8
SB_HEAD_DIM = 64
SB_WIDTH = SB_HEADS * SB_HEAD_DIM
POOL_WINDOWS = (2, 4, 8, 16)
POOL_GROUP_DIM = 128
POOL_WIDTH = len(POOL_WINDOWS) * POOL_GROUP_DIM
POOL_HALO = 16
N_EXPERTS = 32
TOP_K = 4
SWIGLU_LIMIT = 7.0
SWIGLU_ALPHA = 1.702
EXP_UNDERFLOW_F32 = -105.0

LANES = 128
VMEM_LIMIT_BYTES = 56 * 1024 * 1024

PROJ_ROWS = 512
ATTN_BLOCK = 256
ATTN_CHUNK = 128
ATTN_PAIRS = 2
MIX_ROWS = 512
ROW_DMA_GROUP = 8
SCATTER_ROWS = 256
FFN_ROWS = 512
FFN_CHUNKS = 2
COMBINE_ROWS = 256

F32 = jnp.float32
BF16 = jnp.bfloat16


def _dot(a, b):
    return jnp.dot(a, b, preferred_element_type=F32)


def _dot_nt(a, b):
    return lax.dot_general(a, b, (((1,), (1,)), ((), ())), preferred_element_type=F32)


def _split_bf16(x):
    hi = x.astype(BF16)
    lo = (x - hi.astype(F32)).astype(BF16)
    return hi, lo


def _pack_bf16_pair(lo_f32, hi_f32):
    lo_bits = pltpu.bitcast(lo_f32.astype(BF16).astype(F32), jnp.uint32)
    hi_bits = pltpu.bitcast(hi_f32.astype(BF16).astype(F32), jnp.uint32)
    return (hi_bits & jnp.uint32(0xFFFF0000)) | (lo_bits >> 16)


def _unpack_bf16_pair(word):
    lo = pltpu.bitcast(word << 16, F32)
    hi = pltpu.bitcast(word & jnp.uint32(0xFFFF0000), F32)
    return lo, hi


def _in_proj_kernel(x_ref, g_ref, w_ref, q_ref, k_ref, v_ref, u_ref):
    x = x_ref[...]
    ms = jnp.mean(x * x, axis=-1, keepdims=True)
    h = (x * lax.rsqrt(ms + NORM_EPS) * g_ref[...]).astype(BF16)
    proj = _dot(h, w_ref[...])
    scale = 1.0 / math.sqrt(SB_HEAD_DIM)
    q_ref[...] = (proj[:, :SB_WIDTH] * scale).astype(BF16)
    k_ref[...] = proj[:, SB_WIDTH:2 * SB_WIDTH].astype(BF16)
    v_ref[...] = proj[:, 2 * SB_WIDTH:3 * SB_WIDTH].astype(BF16)
    u_ref[...] = proj[:, 3 * SB_WIDTH:].astype(BF16)


def _in_proj(xf, g, w_bf16):
    n, d = xf.shape
    tm = min(PROJ_ROWS, n)
    wide = w_bf16.shape[1]
    out = jax.ShapeDtypeStruct((n, SB_WIDTH), BF16)
    row_spec = pl.BlockSpec((tm, SB_WIDTH), lambda i: (i, 0))
    return pl.pallas_call(
        _in_proj_kernel,
        grid=(n // tm,),
        in_specs=[pl.BlockSpec((tm, d), lambda i: (i, 0)),
                  pl.BlockSpec((1, d), lambda i: (0, 0)),
                  pl.BlockSpec((d, wide), lambda i: (0, 0))],
        out_specs=[row_spec, row_spec, row_spec, row_spec],
        out_shape=[out, out, out, jax.ShapeDtypeStruct((n, POOL_WIDTH), BF16)],
        compiler_params=pltpu.CompilerParams(
            dimension_semantics=("parallel",), vmem_limit_bytes=VMEM_LIMIT_BYTES),
        name="in_proj",
    )(xf, g, w_bf16)


def _attention_kernel(q_ref, k_ref, v_ref, g_ref, o_ref, acc_ref, carry_ref):
    tq = q_ref.shape[0]
    tk = tq
    n_pairs = q_ref.shape[1] // LANES
    rows = 2 * n_pairs * tq
    n_chunks = rows // ATTN_CHUNK
    qi = pl.program_id(2)
    lane = lax.broadcasted_iota(jnp.int32, (tq, LANES), 1)
    stacked = []
    for p in range(n_pairs):
        q = q_ref[:, pl.ds(p * LANES, LANES)].astype(F32)
        stacked += [jnp.where(lane < SB_HEAD_DIM, q, 0.0), jnp.where(lane >= SB_HEAD_DIM, q, 0.0)]
    q_stack = jnp.concatenate(stacked, axis=0).astype(BF16)

    key_row = lax.broadcasted_iota(jnp.int32, (tk, tk), 0)
    key_col = lax.broadcasted_iota(jnp.int32, (tk, tk), 1)
    later = (key_row > key_col).astype(BF16)

    chunks = [slice(c * ATTN_CHUNK, (c + 1) * ATTN_CHUNK) for c in range(n_chunks)]

    def past_mask(c, keys):
        query = (c.start % tq) + lax.broadcasted_iota(jnp.int32, (ATTN_CHUNK, keys), 0)
        return lax.broadcasted_iota(jnp.int32, (ATTN_CHUNK, keys), 1) < query

    def visit(offsets, carry, first_is_diagonal):
        blocks = [[(k_ref[pl.ds(off, tk), pl.ds(p * LANES, LANES)],
                    v_ref[pl.ds(off, tk), pl.ds(p * LANES, LANES)]) for p in range(n_pairs)]
                  for off in offsets]
        pair = lambda c: c.start // (2 * tq)
        chains = [(b, c) for b in range(len(blocks)) for c in chunks]
        masked = lambda b: first_is_diagonal and b == 0
        keys = [min(tk, c.start % tq + ATTN_CHUNK) if masked(b) else tk for b, c in chains]
        z = [_dot_nt(q_stack[c], blocks[b][pair(c)][0][:n]) for (b, c), n in zip(chains, keys)]
        log_keep, split = [], []
        for (b, c), zc, n in zip(chains, z, keys):
            lk = -(jnp.maximum(zc, 0.0) + jnp.log(1.0 + jnp.exp(-jnp.abs(zc))))
            if masked(b):
                lk = jnp.where(past_mask(c, n), lk, 0.0)
            log_keep.append(lk)
            split.append(jnp.concatenate(_split_bf16(lk), axis=0))
        sums = [_dot(sp, later[:n, :n]) for sp, n in zip(split, keys)]
        carries = {c.start: (None if carry is None else carry[c]) for c in chunks}
        for i, (b, c) in enumerate(chains):
            after = sums[i][:ATTN_CHUNK] + sums[i][ATTN_CHUNK:]
            if carries[c.start] is not None:
                after = after + carries[c.start]
            a = jnp.exp(z[i] + log_keep[i] + after)
            if masked(b):
                a = jnp.where(past_mask(c, keys[i]), a, 0.0)
            acc_ref[c, :] += _dot(a.astype(BF16), blocks[b][pair(c)][1][:keys[i]])
            total = jnp.sum(log_keep[i], axis=-1, keepdims=True)
            carries[c.start] = total if carries[c.start] is None else carries[c.start] + total
        return jnp.concatenate([carries[c.start] for c in chunks], axis=0)

    acc_ref[...] = jnp.zeros_like(acc_ref)
    start = pl.multiple_of(qi * tk, tk)

    @pl.when(qi == 0)
    def _():
        carry_ref[...] = visit([start], None, first_is_diagonal=True)

    @pl.when(qi > 0)
    def _():
        carry_ref[...] = visit([start, pl.multiple_of(start - tk, tk)], None, first_is_diagonal=True)

    def more(state):
        step, carry = state
        return (step < qi) & (jnp.max(carry) >= EXP_UNDERFLOW_F32)

    def body(state):
        step, carry = state
        off = pl.multiple_of((qi - 1 - step) * tk, tk)
        return step + 1, visit([off], carry, first_is_diagonal=False)

    lax.while_loop(more, body, (jnp.minimum(qi, 1), carry_ref[...]))

    for p in range(n_pairs):
        o = jnp.where(lane < SB_HEAD_DIM, acc_ref[pl.ds(2 * p * tq, tq), :],
                      acc_ref[pl.ds((2 * p + 1) * tq, tq), :])
        sq = o * o
        s0 = jnp.sum(jnp.where(lane < SB_HEAD_DIM, sq, 0.0), axis=-1, keepdims=True)
        s1 = jnp.sum(jnp.where(lane >= SB_HEAD_DIM, sq, 0.0), axis=-1, keepdims=True)
        ms = jnp.where(lane < SB_HEAD_DIM, s0, s1) * (1.0 / SB_HEAD_DIM)
        cols = pl.ds(p * LANES, LANES)
        o_ref[:, cols] = (o * lax.rsqrt(ms + NORM_EPS) * g_ref[:, cols]).astype(o_ref.dtype)


def _attention(q, k, v, sb_g):
    b, s, _ = q.shape
    t = min(ATTN_BLOCK, s)
    width = LANES * ATTN_PAIRS
    blk = pl.BlockSpec((None, t, width), lambda bi, p, i: (bi, i, p))
    seq = pl.BlockSpec((None, s, width), lambda bi, p, i: (bi, 0, p))
    return pl.pallas_call(
        _attention_kernel,
        grid=(b, SB_WIDTH // width, s // t),
        in_specs=[blk, seq, seq, pl.BlockSpec((1, width), lambda bi, p, i: (0, p))],
        out_specs=blk,
        out_shape=jax.ShapeDtypeStruct((b, s, SB_WIDTH), BF16),
        scratch_shapes=[pltpu.VMEM((2 * ATTN_PAIRS * t, LANES), F32),
                        pltpu.VMEM((2 * ATTN_PAIRS * t, 1), F32)],
        compiler_params=pltpu.CompilerParams(
            dimension_semantics=("parallel", "parallel", "arbitrary"),
            vmem_limit_bytes=VMEM_LIMIT_BYTES),
        name="attention",
    )(q, k, v, sb_g)


def _mix_route_kernel(x_ref, sb_ref, u_ref, halo_ref, wout_ref, wpool_ref, pscale_ref, g2_ref,
                      wr_ref, br_ref,
                      x1_ref, h2p_ref, idx_ref, gate_ref, rank_ref, count_ref,
                      count_acc, *, tiles_per_seq):
    tm = x_ref.shape[0]
    i = pl.program_id(0)
    seq_tile = i % tiles_per_seq

    t_main = lax.broadcasted_iota(jnp.int32, (tm, tm), 0)
    s_main = lax.broadcasted_iota(jnp.int32, (tm, tm), 1)
    t_halo = lax.broadcasted_iota(jnp.int32, (tm, POOL_HALO), 0)
    s_halo = lax.broadcasted_iota(jnp.int32, (tm, POOL_HALO), 1) - POOL_HALO
    pos = seq_tile * tm + lax.broadcasted_iota(jnp.int32, (tm, 1), 0)
    u = u_ref[...]
    halo = jnp.where(seq_tile > 0, halo_ref[...], jnp.zeros_like(halo_ref))
    pooled_groups = []
    for g, w in enumerate(POOL_WINDOWS):
        cols = pl.ds(g * POOL_GROUP_DIM, POOL_GROUP_DIM)
        band_main = ((s_main <= t_main) & (s_main > t_main - w)).astype(BF16)
        band_halo = (s_halo > t_halo - w).astype(BF16)
        p = u[:, g * POOL_GROUP_DIM:(g + 1) * POOL_GROUP_DIM]
        win = _dot(band_main, p) + _dot(band_halo, halo[:, g * POOL_GROUP_DIM:(g + 1) * POOL_GROUP_DIM])
        cnt = jnp.minimum(pos + 1, w).astype(F32)
        mix = win / cnt - p.astype(F32)
        pooled = _dot(mix.astype(BF16), wpool_ref[g]) * pscale_ref[:, cols]
        pooled_groups.append(pooled.astype(BF16))
    mixed = jnp.concatenate([sb_ref[...]] + pooled_groups, axis=1)
    y = _dot(mixed, wout_ref[...])

    x1 = x_ref[...] + y
    x1_ref[...] = x1
    ms = jnp.mean(x1 * x1, axis=-1, keepdims=True)
    h2 = x1 * lax.rsqrt(ms + NORM_EPS) * g2_ref[...]
    half = h2.shape[1] // 2
    h2p_ref[...] = _pack_bf16_pair(h2[:, :half], h2[:, half:])

    h_hi, h_lo = _split_bf16(h2)
    w_hi = wr_ref[0]
    w_lo = wr_ref[1]
    logits = _dot_nt(w_hi, h_hi) + _dot_nt(w_hi, h_lo) + _dot_nt(w_lo, h_hi) + br_ref[...]

    eidx = lax.broadcasted_iota(jnp.int32, (N_EXPERTS, tm), 0)
    work = logits
    vals, ids, hots = [], [], []
    for _ in range(TOP_K):
        m = jnp.max(work, axis=0, keepdims=True)
        sel = jnp.min(jnp.where(work == m, eidx, N_EXPERTS), axis=0, keepdims=True)
        hot = eidx == sel
        work = jnp.where(hot, -jnp.inf, work)
        vals.append(m)
        ids.append(sel)
        hots.append(hot)
    exps = [jnp.exp(v - vals[0]) for v in vals]
    denom = exps[0] + exps[1] + exps[2] + exps[3]

    @pl.when(i == 0)
    def _():
        count_acc[...] = jnp.zeros_like(count_acc)

    chosen = hots[0] | hots[1] | hots[2] | hots[3]
    earlier = (t_main < s_main).astype(BF16)
    before = _dot(chosen.astype(BF16), earlier) + count_acc[:, 0:1]
    for r in range(TOP_K):
        idx_ref[r:r + 1, :] = ids[r]
        gate_ref[r:r + 1, :] = exps[r] / denom
        rank = jnp.sum(jnp.where(hots[r], before, 0.0), axis=0, keepdims=True)
        rank_ref[r:r + 1, :] = rank.astype(jnp.int32)
    count_acc[...] += jnp.sum(chosen.astype(F32), axis=1, keepdims=True)
    count_ref[...] = count_acc[...]


def _mix_route(xf, sb, u, w_out, w_pool, pool_scale, g2, wr_split, b_router, seq_len):
    n, d = xf.shape
    tm = min(MIX_ROWS, seq_len)
    tiles_per_seq = seq_len // tm
    halo_blocks = tm // POOL_HALO
    row = lambda width: pl.BlockSpec((tm, width), lambda i: (i, 0))
    full = lambda shape: pl.BlockSpec(shape, lambda i: (0,) * len(shape))
    tok = pl.BlockSpec((TOP_K, tm), lambda i: (0, i))
    return pl.pallas_call(
        functools.partial(_mix_route_kernel, tiles_per_seq=tiles_per_seq),
        grid=(n // tm,),
        in_specs=[row(d), row(SB_WIDTH), row(POOL_WIDTH),
                  pl.BlockSpec((POOL_HALO, POOL_WIDTH), lambda i: (jnp.maximum(i * halo_blocks - 1, 0), 0)),
                  full(w_out.shape), full(w_pool.shape), full(pool_scale.shape), full(g2.shape),
                  full(wr_split.shape), full(b_router.shape)],
        out_specs=[row(d), row(d // 2), tok, tok, tok, full((N_EXPERTS, LANES))],
        out_shape=[jax.ShapeDtypeStruct((n, d), F32),
                   jax.ShapeDtypeStruct((n, d // 2), jnp.uint32),
                   jax.ShapeDtypeStruct((TOP_K, n), jnp.int32),
                   jax.ShapeDtypeStruct((TOP_K, n), F32),
                   jax.ShapeDtypeStruct((TOP_K, n), jnp.int32),
                   jax.ShapeDtypeStruct((N_EXPERTS, LANES), F32)],
        scratch_shapes=[pltpu.VMEM((N_EXPERTS, LANES), F32)],
        compiler_params=pltpu.CompilerParams(
            dimension_semantics=("arbitrary",), vmem_limit_bytes=VMEM_LIMIT_BYTES),
        name="mix_route",
    )(xf, sb, u, u, w_out, w_pool, pool_scale, g2, wr_split, b_router)


def _for_each_row_dma(tm, make_copy, action):
    def group(g, c):
        for j in range(ROW_DMA_GROUP):
            for r in range(TOP_K):
                action(make_copy(g, j, r, (g * ROW_DMA_GROUP + j) * TOP_K + r), r)
        return c

    lax.fori_loop(0, tm // ROW_DMA_GROUP, group, 0)


def _start_row_dma(copy, r):
    copy.start(priority=r % 2)


def _wait_row_dma(copy, r):
    copy.wait()


def _scatter_rows_kernel(dest_ref, h_ref, buf_in_ref, buf_ref, sem):
    del buf_in_ref
    tm = h_ref.shape[0] * ROW_DMA_GROUP

    def copy(g, j, r, flat):
        return pltpu.make_async_copy(h_ref.at[g, pl.ds(j, 1), :],
                                     buf_ref.at[pl.ds(dest_ref[flat], 1), :], sem)

    _for_each_row_dma(tm, copy, _start_row_dma)
    _for_each_row_dma(tm, copy, _wait_row_dma)


def _scatter_rows(dest_flat, h2p, n_rows):
    n, width = h2p.shape
    tm = min(SCATTER_ROWS, n)
    zeros = jnp.zeros((n_rows, width), h2p.dtype)
    return pl.pallas_call(
        _scatter_rows_kernel,
        grid=(n // tm,),
        in_specs=[pl.BlockSpec((tm * TOP_K,), lambda i: (i,), memory_space=pltpu.SMEM),
                  pl.BlockSpec((tm // ROW_DMA_GROUP, ROW_DMA_GROUP, width), lambda i: (i, 0, 0)),
                  pl.BlockSpec(memory_space=pl.ANY)],
        out_specs=pl.BlockSpec(memory_space=pl.ANY),
        out_shape=jax.ShapeDtypeStruct((n_rows, width), h2p.dtype),
        scratch_shapes=[pltpu.SemaphoreType.DMA(())],
        input_output_aliases={2: 0},
        compiler_params=pltpu.CompilerParams(
            dimension_semantics=("arbitrary",), vmem_limit_bytes=VMEM_LIMIT_BYTES),
        name="scatter_rows",
    )(dest_flat, h2p.reshape(n // ROW_DMA_GROUP, ROW_DMA_GROUP, width), zeros)


def _expert_ffn_kernel(block_e_ref, n_used_ref, first_ref, slot_ref, next_e_ref,
                       x_ref, wgu_hbm, bgu_ref, wd_hbm, bd_ref, y_ref,
                       wgu_f32, wd_f32, wgu_bf, wd_bf, sem):
    i = pl.program_id(0)
    d_ff = wd_bf.shape[0]
    half = x_ref.shape[1]

    def fetch(e, slot):
        return (pltpu.make_async_copy(wgu_hbm.at[e], wgu_f32.at[slot], sem.at[0, slot]),
                pltpu.make_async_copy(wd_hbm.at[e], wd_f32.at[slot], sem.at[1, slot]))

    @pl.when(i < n_used_ref[0])
    def _():
        e = block_e_ref[i]
        slot = slot_ref[i]

        @pl.when(i == 0)
        def _():
            for copy in fetch(e, slot):
                copy.start()

        @pl.when(first_ref[i] == 1)
        def _():
            for copy in fetch(e, slot):
                copy.wait()
            wgu_bf[...] = wgu_f32[slot].astype(BF16)
            wd_bf[...] = wd_f32[slot].astype(BF16)

            @pl.when(next_e_ref[i] >= 0)
            def _():
                for copy in fetch(next_e_ref[i], 1 - slot):
                    copy.start()

        chunk_rows = x_ref.shape[0] // FFN_CHUNKS
        chunks = [pl.ds(c * chunk_rows, chunk_rows) for c in range(FFN_CHUNKS)]
        gus = []
        for rows in chunks:
            x_lo, x_hi = _unpack_bf16_pair(x_ref[rows, :])
            gus.append(_dot(x_lo.astype(BF16), wgu_bf[pl.ds(0, half), :])
                       + _dot(x_hi.astype(BF16), wgu_bf[pl.ds(half, half), :]) + bgu_ref[...])
        for rows, gu in zip(chunks, gus):
            gate = jnp.minimum(gu[:, :d_ff], SWIGLU_LIMIT)
            up = jnp.clip(gu[:, d_ff:], -SWIGLU_LIMIT, SWIGLU_LIMIT)
            act = (up + 1.0) * gate * (1.0 / (1.0 + jnp.exp(-SWIGLU_ALPHA * gate)))
            y = _dot(act.astype(BF16), wd_bf[...]) + bd_ref[...]
            y_ref[rows, :] = _pack_bf16_pair(y[:, :half], y[:, half:])

    @pl.when(i >= n_used_ref[0])
    def _():
        y_ref[...] = jnp.zeros_like(y_ref)


def _expert_ffn(block_e, n_used, buf, w_gate_up, b_gate_up, w_down, b_down):
    n_rows, half = buf.shape
    _, d, two_f = w_gate_up.shape
    d_ff = w_down.shape[1]
    n_blocks = n_rows // FFN_ROWS

    index = jnp.arange(n_blocks, dtype=jnp.int32)
    used = index < n_used[0]
    first = used & ((index == 0) | (block_e != jnp.roll(block_e, 1)))
    slot = (jnp.cumsum(first.astype(jnp.int32)) - 1) % 2
    first_at = jnp.where(first, index, n_blocks)
    next_first = lax.cummin(jnp.concatenate([first_at[1:], jnp.full((1,), n_blocks, jnp.int32)]),
                            reverse=True)
    next_e = jnp.where(next_first < n_blocks, block_e[jnp.minimum(next_first, n_blocks - 1)], -1)

    def rows(i, be, nu, *_):
        return (jnp.minimum(i, nu[0] - 1), 0)

    def expert(i, be, nu, *_):
        return (be[jnp.minimum(i, nu[0] - 1)], 0, 0)

    grid_spec = pltpu.PrefetchScalarGridSpec(
        num_scalar_prefetch=5,
        grid=(n_blocks,),
        in_specs=[pl.BlockSpec((FFN_ROWS, half), rows),
                  pl.BlockSpec(memory_space=pl.ANY),
                  pl.BlockSpec((None, 1, two_f), expert),
                  pl.BlockSpec(memory_space=pl.ANY),
                  pl.BlockSpec((None, 1, d), expert)],
        out_specs=pl.BlockSpec((FFN_ROWS, half), lambda i, *_: (i, 0)),
        scratch_shapes=[pltpu.VMEM((2, d, two_f), F32), pltpu.VMEM((2, d_ff, d), F32),
                        pltpu.VMEM((d, two_f), BF16), pltpu.VMEM((d_ff, d), BF16),
                        pltpu.SemaphoreType.DMA((2, 2))],
    )
    return pl.pallas_call(
        _expert_ffn_kernel,
        grid_spec=grid_spec,
        out_shape=jax.ShapeDtypeStruct((n_rows, half), jnp.uint32),
        compiler_params=pltpu.CompilerParams(
            dimension_semantics=("arbitrary",), vmem_limit_bytes=VMEM_LIMIT_BYTES),
        name="expert_ffn",
    )(block_e, n_used, first.astype(jnp.int32), slot.astype(jnp.int32), next_e.astype(jnp.int32),
      buf, w_gate_up, b_gate_up, w_down, b_down)


def _combine_kernel(dest_ref, dest_next_ref, x1_ref, gate_ref, g_ref, y_ref, o_ref, rows_ref, sem):
    tm = rows_ref.shape[2] * ROW_DMA_GROUP
    half = rows_ref.shape[4]
    step = pl.program_id(0)

    def gather(table_ref, table_offset, slot):
        def copy(g, j, r, flat):
            return pltpu.make_async_copy(y_ref.at[pl.ds(table_ref[table_offset + flat], 1), :],
                                         rows_ref.at[slot, r, g, pl.ds(j, 1), :], sem.at[slot])
        return copy

    def finish(slot):
        _for_each_row_dma(tm, gather(dest_ref, slot * tm * TOP_K, slot), _wait_row_dma)
        rows = pl.ds(slot * tm, tm)
        gates = gate_ref[rows, :]
        lo_sum = x1_ref[rows, pl.ds(0, half)]
        hi_sum = x1_ref[rows, pl.ds(half, half)]
        for r in range(TOP_K):
            lo, hi = _unpack_bf16_pair(rows_ref[slot, r].reshape(tm, half))
            g = gates[:, r:r + 1]
            lo_sum += g * lo
            hi_sum += g * hi
        ms = (jnp.sum(lo_sum * lo_sum, axis=-1, keepdims=True)
              + jnp.sum(hi_sum * hi_sum, axis=-1, keepdims=True)) * (1.0 / (2 * half))
        inv = lax.rsqrt(ms + NORM_EPS)
        o_ref[rows, pl.ds(0, half)] = lo_sum * inv * g_ref[:, pl.ds(0, half)]
        o_ref[rows, pl.ds(half, half)] = hi_sum * inv * g_ref[:, pl.ds(half, half)]

    @pl.when(step == 0)
    def _():
        _for_each_row_dma(tm, gather(dest_ref, 0, 0), _start_row_dma)

    _for_each_row_dma(tm, gather(dest_ref, tm * TOP_K, 1), _start_row_dma)
    finish(0)

    @pl.when(step + 1 < pl.num_programs(0))
    def _():
        _for_each_row_dma(tm, gather(dest_next_ref, 0, 0), _start_row_dma)

    finish(1)


def _combine(dest_flat, x1, gates_t, final_g, y):
    n, d = x1.shape
    tm = min(COMBINE_ROWS, n // 2)
    steps = n // (2 * tm)
    table = lambda index_map: pl.BlockSpec((2 * tm * TOP_K,), index_map, memory_space=pltpu.SMEM)
    return pl.pallas_call(
        _combine_kernel,
        grid=(steps,),
        in_specs=[table(lambda i: (i,)),
                  table(lambda i: (jnp.minimum(i + 1, steps - 1),)),
                  pl.BlockSpec((2 * tm, d), lambda i: (i, 0)),
                  pl.BlockSpec((2 * tm, TOP_K), lambda i: (i, 0)),
                  pl.BlockSpec((1, d), lambda i: (0, 0)),
                  pl.BlockSpec(memory_space=pl.ANY)],
        out_specs=pl.BlockSpec((2 * tm, d), lambda i: (i, 0)),
        out_shape=jax.ShapeDtypeStruct((n, d), F32),
        scratch_shapes=[pltpu.VMEM((2, TOP_K, tm // ROW_DMA_GROUP, ROW_DMA_GROUP, d // 2), jnp.uint32),
                        pltpu.SemaphoreType.DMA((2,))],
        compiler_params=pltpu.CompilerParams(
            dimension_semantics=("arbitrary",), vmem_limit_bytes=VMEM_LIMIT_BYTES),
        name="combine",
    )(dest_flat, dest_flat, x1, gates_t, final_g, y)


def _layer(x, norm1_g, w_in, sb_norm_g, w_pool, pool_scale, w_out, norm2_g,
           w_router, b_router, w_gate_up, b_gate_up, w_down, b_down, out_g):
    b, s, d = x.shape
    n = b * s
    xf = x.reshape(n, d)

    q, k, v, u = _in_proj(xf, norm1_g.reshape(1, d), w_in.astype(BF16))
    sb = _attention(q.reshape(b, s, SB_WIDTH), k.reshape(b, s, SB_WIDTH), v.reshape(b, s, SB_WIDTH),
                    sb_norm_g.reshape(1, SB_WIDTH))

    wr_t = w_router.T
    wr_hi = wr_t.astype(BF16)
    wr_lo = (wr_t - wr_hi.astype(F32)).astype(BF16)
    x1, h2p, idx, gates, rank, counts = _mix_route(
        xf, sb.reshape(n, SB_WIDTH), u, w_out.astype(BF16), w_pool.astype(BF16),
        pool_scale.reshape(1, POOL_WIDTH), norm2_g.reshape(1, d),
        jnp.stack([wr_hi, wr_lo]), b_router.reshape(N_EXPERTS, 1), s)

    counts = counts[:, 0].astype(jnp.int32)
    padded = (counts + FFN_ROWS - 1) // FFN_ROWS * FFN_ROWS
    pend = jnp.cumsum(padded)
    pstart = pend - padded
    experts = jnp.arange(N_EXPERTS, dtype=jnp.int32)[:, None, None]
    dest = rank + jnp.sum(jnp.where(idx[None] == experts, pstart[:, None, None], 0), axis=0)
    n_blocks = -(-(n * TOP_K) // FFN_ROWS) + N_EXPERTS
    n_used = (pend[-1] // FFN_ROWS).astype(jnp.int32).reshape(1)
    block_start = jnp.arange(n_blocks, dtype=jnp.int32) * FFN_ROWS
    block_e = jnp.minimum(jnp.sum(pend[None, :] <= block_start[:, None], axis=1),
                          N_EXPERTS - 1).astype(jnp.int32)

    dest_flat = dest.T.reshape(-1)
    buf = _scatter_rows(dest_flat, h2p, n_blocks * FFN_ROWS)
    y = _expert_ffn(block_e, n_used, buf, w_gate_up, b_gate_up.reshape(N_EXPERTS, 1, -1),
                    w_down, b_down.reshape(N_EXPERTS, 1, -1))
    out = _combine(dest_flat, x1, gates.T, out_g.reshape(1, d), y)
    return out.reshape(b, s, d)


def kernel(x, norm1_g, w_in, sb_norm_g, w_pool, pool_scale, w_out, norm2_g, w_router, b_router,
           w_gate_up, b_gate_up, w_down, b_down, final_norm_g):
    assert norm1_g.shape[0] == 1, "single-layer block"
    return _layer(x, norm1_g[0], w_in[0], sb_norm_g[0], w_pool[0], pool_scale[0], w_out[0],
                  norm2_g[0], w_router[0], b_router[0], w_gate_up[0], b_gate_up[0], w_down[0],
                  b_down[0], final_norm_g)
```

```python
import functools
import math

import jax
import jax.numpy as jnp
from jax import lax
from jax.experimental import pallas as pl
from jax.experimental.pallas import tpu as pltpu

NORM_EPS = 1e-5
SB_HEADS = 8
SB_HEAD_DIM = 64
SB_WIDTH = SB_HEADS * SB_HEAD_DIM
POOL_WINDOWS = (2, 4, 8, 16)
POOL_GROUP_DIM = 128
POOL_WIDTH = len(POOL_WINDOWS) * POOL_GROUP_DIM
POOL_HALO = 16
N_EXPERTS = 32
TOP_K = 4
SWIGLU_LIMIT = 7.0
SWIGLU_ALPHA = 1.702
EXP_UNDERFLOW_F32 = -105.0

LANES = 128
VMEM_LIMIT_BYTES = 56 * 1024 * 1024

PROJ_ROWS = 512
ATTN_BLOCK = 256
ATTN_CHUNK = 128
ATTN_PAIRS = 2
MIX_ROWS = 512
ROW_DMA_GROUP = 8
SCATTER_ROWS = 256
FFN_ROWS = 512
FFN_CHUNKS = 2
COMBINE_ROWS = 256

F32 = jnp.float32
BF16 = jnp.bfloat16


def _dot(a, b):
    return jnp.dot(a, b, preferred_element_type=F32)


def _dot_nt(a, b):
    return lax.dot_general(a, b, (((1,), (1,)), ((), ())), preferred_element_type=F32)


def _split_bf16(x):
    hi = x.astype(BF16)
    lo = (x - hi.astype(F32)).astype(BF16)
    return hi, lo


def _pack_bf16_pair(lo_f32, hi_f32):
    lo_bits = pltpu.bitcast(lo_f32.astype(BF16).astype(F32), jnp.uint32)
    hi_bits = pltpu.bitcast(hi_f32.astype(BF16).astype(F32), jnp.uint32)
    return (hi_bits & jnp.uint32(0xFFFF0000)) | (lo_bits >> 16)


def _unpack_bf16_pair(word):
    lo = pltpu.bitcast(word << 16, F32)
    hi = pltpu.bitcast(word & jnp.uint32(0xFFFF0000), F32)
    return lo, hi


def _in_proj_kernel(x_ref, g_ref, w_ref, q_ref, k_ref, v_ref, u_ref):
    x = x_ref[...]
    ms = jnp.mean(x * x, axis=-1, keepdims=True)
    h = (x * lax.rsqrt(ms + NORM_EPS) * g_ref[...]).astype(BF16)
    proj = _dot(h, w_ref[...])
    scale = 1.0 / math.sqrt(SB_HEAD_DIM)
    q_ref[...] = (proj[:, :SB_WIDTH] * scale).astype(BF16)
    k_ref[...] = proj[:, SB_WIDTH:2 * SB_WIDTH].astype(BF16)
    v_ref[...] = proj[:, 2 * SB_WIDTH:3 * SB_WIDTH].astype(BF16)
    u_ref[...] = proj[:, 3 * SB_WIDTH:].astype(BF16)


def _in_proj(xf, g, w_bf16):
    n, d = xf.shape
    tm = min(PROJ_ROWS, n)
    wide = w_bf16.shape[1]
    out = jax.ShapeDtypeStruct((n, SB_WIDTH), BF16)
    row_spec = pl.BlockSpec((tm, SB_WIDTH), lambda i: (i, 0))
    return pl.pallas_call(
        _in_proj_kernel,
        grid=(n // tm,),
        in_specs=[pl.BlockSpec((tm, d), lambda i: (i, 0)),
                  pl.BlockSpec((1, d), lambda i: (0, 0)),
                  pl.BlockSpec((d, wide), lambda i: (0, 0))],
        out_specs=[row_spec, row_spec, row_spec, row_spec],
        out_shape=[out, out, out, jax.ShapeDtypeStruct((n, POOL_WIDTH), BF16)],
        compiler_params=pltpu.CompilerParams(
            dimension_semantics=("parallel",), vmem_limit_bytes=VMEM_LIMIT_BYTES),
        name="in_proj",
    )(xf, g, w_bf16)


def _attention_kernel(q_ref, k_ref, v_ref, g_ref, o_ref, acc_ref, carry_ref):
    tq = q_ref.shape[0]
    tk = tq
    n_pairs = q_ref.shape[1] // LANES
    rows = 2 * n_pairs * tq
    n_chunks = rows // ATTN_CHUNK
    qi = pl.program_id(2)
    lane = lax.broadcasted_iota(jnp.int32, (tq, LANES), 1)
    stacked = []
    for p in range(n_pairs):
        q = q_ref[:, pl.ds(p * LANES, LANES)].astype(F32)
        stacked += [jnp.where(lane < SB_HEAD_DIM, q, 0.0), jnp.where(lane >= SB_HEAD_DIM, q, 0.0)]
    q_stack = jnp.concatenate(stacked, axis=0).astype(BF16)

    key_row = lax.broadcasted_iota(jnp.int32, (tk, tk), 0)
    key_col = lax.broadcasted_iota(jnp.int32, (tk, tk), 1)
    later = (key_row > key_col).astype(BF16)

    chunks = [slice(c * ATTN_CHUNK, (c + 1) * ATTN_CHUNK) for c in range(n_chunks)]

    def past_mask(c, keys):
        query = (c.start % tq) + lax.broadcasted_iota(jnp.int32, (ATTN_CHUNK, keys), 0)
        return lax.broadcasted_iota(jnp.int32, (ATTN_CHUNK, keys), 1) < query

    def visit(offsets, carry, first_is_diagonal):
        blocks = [[(k_ref[pl.ds(off, tk), pl.ds(p * LANES, LANES)],
                    v_ref[pl.ds(off, tk), pl.ds(p * LANES, LANES)]) for p in range(n_pairs)]
                  for off in offsets]
        pair = lambda c: c.start // (2 * tq)
        chains = [(b, c) for b in range(len(blocks)) for c in chunks]
        masked = lambda b: first_is_diagonal and b == 0
        keys = [min(tk, c.start % tq + ATTN_CHUNK) if masked(b) else tk for b, c in chains]
        z = [_dot_nt(q_stack[c], blocks[b][pair(c)][0][:n]) for (b, c), n in zip(chains, keys)]
        log_keep, split = [], []
        for (b, c), zc, n in zip(chains, z, keys):
            lk = -(jnp.maximum(zc, 0.0) + jnp.log(1.0 + jnp.exp(-jnp.abs(zc))))
            if masked(b):
                lk = jnp.where(past_mask(c, n), lk, 0.0)
            log_keep.append(lk)
            split.append(jnp.concatenate(_split_bf16(lk), axis=0))
        sums = [_dot(sp, later[:n, :n]) for sp, n in zip(split, keys)]
        carries = {c.start: (None if carry is None else carry[c]) for c in chunks}
        for i, (b, c) in enumerate(chains):
            after = sums[i][:ATTN_CHUNK] + sums[i][ATTN_CHUNK:]
            if carries[c.start] is not None:
                after = after + carries[c.start]
            a = jnp.exp(z[i] + log_keep[i] + after)
            if masked(b):
                a = jnp.where(past_mask(c, keys[i]), a, 0.0)
            values = _dot(a.astype(BF16), blocks[b][pair(c)][1][:keys[i]])
            if carry is None and b == 0:
                acc_ref[c, :] = values
            else:
                acc_ref[c, :] += values
            total = jnp.sum(log_keep[i], axis=-1, keepdims=True)
            carries[c.start] = total if carries[c.start] is None else carries[c.start] + total
        return jnp.concatenate([carries[c.start] for c in chunks], axis=0)

    start = pl.multiple_of(qi * tk, tk)

    @pl.when(qi == 0)
    def _():
        carry_ref[...] = visit([start], None, first_is_diagonal=True)

    @pl.when(qi > 0)
    def _():
        carry_ref[...] = visit([start, pl.multiple_of(start - tk, tk)], None, first_is_diagonal=True)

    def more(state):
        step, carry = state
        return (step < qi) & (jnp.max(carry) >= EXP_UNDERFLOW_F32)

    def body(state):
        step, carry = state
        off = pl.multiple_of((qi - 1 - step) * tk, tk)
        return step + 1, visit([off], carry, first_is_diagonal=False)

    lax.while_loop(more, body, (jnp.minimum(qi, 1), carry_ref[...]))

    for p in range(n_pairs):
        o = jnp.where(lane < SB_HEAD_DIM, acc_ref[pl.ds(2 * p * tq, tq), :],
                      acc_ref[pl.ds((2 * p + 1) * tq, tq), :])
        sq = o * o
        s0 = jnp.sum(jnp.where(lane < SB_HEAD_DIM, sq, 0.0), axis=-1, keepdims=True)
        s1 = jnp.sum(jnp.where(lane >= SB_HEAD_DIM, sq, 0.0), axis=-1, keepdims=True)
        ms = jnp.where(lane < SB_HEAD_DIM, s0, s1) * (1.0 / SB_HEAD_DIM)
        cols = pl.ds(p * LANES, LANES)
        o_ref[:, cols] = (o * lax.rsqrt(ms + NORM_EPS) * g_ref[:, cols]).astype(o_ref.dtype)


def _attention(q, k, v, sb_g):
    b, s, _ = q.shape
    t = min(ATTN_BLOCK, s)
    width = LANES * ATTN_PAIRS
    blk = pl.BlockSpec((None, t, width), lambda bi, p, i: (bi, i, p))
    seq = pl.BlockSpec((None, s, width), lambda bi, p, i: (bi, 0, p))
    return pl.pallas_call(
        _attention_kernel,
        grid=(b, SB_WIDTH // width, s // t),
        in_specs=[blk, seq, seq, pl.BlockSpec((1, width), lambda bi, p, i: (0, p))],
        out_specs=blk,
        out_shape=jax.ShapeDtypeStruct((b, s, SB_WIDTH), BF16),
        scratch_shapes=[pltpu.VMEM((2 * ATTN_PAIRS * t, LANES), F32),
                        pltpu.VMEM((2 * ATTN_PAIRS * t, 1), F32)],
        compiler_params=pltpu.CompilerParams(
            dimension_semantics=("parallel", "parallel", "arbitrary"),
            vmem_limit_bytes=VMEM_LIMIT_BYTES),
        name="attention",
    )(q, k, v, sb_g)


def _mix_route_kernel(x_ref, sb_ref, u_ref, halo_ref, wout_ref, wpool_ref, pscale_ref, g2_ref,
                      wr_ref, br_ref,
                      x1_ref, h2p_ref, idx_ref, gate_ref, rank_ref, count_ref,
                      count_acc, *, tiles_per_seq):
    tm = x_ref.shape[0]
    i = pl.program_id(0)
    seq_tile = i % tiles_per_seq

    t_main = lax.broadcasted_iota(jnp.int32, (tm, tm), 0)
    s_main = lax.broadcasted_iota(jnp.int32, (tm, tm), 1)
    t_halo = lax.broadcasted_iota(jnp.int32, (tm, POOL_HALO), 0)
    s_halo = lax.broadcasted_iota(jnp.int32, (tm, POOL_HALO), 1) - POOL_HALO
    pos = seq_tile * tm + lax.broadcasted_iota(jnp.int32, (tm, 1), 0)
    u = u_ref[...]
    halo = jnp.where(seq_tile > 0, halo_ref[...], jnp.zeros_like(halo_ref))
    pooled_groups = []
    for g, w in enumerate(POOL_WINDOWS):
        cols = pl.ds(g * POOL_GROUP_DIM, POOL_GROUP_DIM)
        band_main = ((s_main <= t_main) & (s_main > t_main - w)).astype(BF16)
        band_halo = (s_halo > t_halo - w).astype(BF16)
        p = u[:, g * POOL_GROUP_DIM:(g + 1) * POOL_GROUP_DIM]
        win = _dot(band_main, p) + _dot(band_halo, halo[:, g * POOL_GROUP_DIM:(g + 1) * POOL_GROUP_DIM])
        cnt = jnp.minimum(pos + 1, w).astype(F32)
        mix = win / cnt - p.astype(F32)
        pooled = _dot(mix.astype(BF16), wpool_ref[g]) * pscale_ref[:, cols]
        pooled_groups.append(pooled.astype(BF16))
    mixed = jnp.concatenate([sb_ref[...]] + pooled_groups, axis=1)
    y = _dot(mixed, wout_ref[...])

    x1 = x_ref[...] + y
    x1_ref[...] = x1
    ms = jnp.mean(x1 * x1, axis=-1, keepdims=True)
    h2 = x1 * lax.rsqrt(ms + NORM_EPS) * g2_ref[...]
    half = h2.shape[1] // 2
    h2p_ref[...] = _pack_bf16_pair(h2[:, :half], h2[:, half:])

    h_hi, h_lo = _split_bf16(h2)
    w_hi = wr_ref[0]
    w_lo = wr_ref[1]
    logits = _dot_nt(w_hi, h_hi) + _dot_nt(w_hi, h_lo) + _dot_nt(w_lo, h_hi) + br_ref[...]

    eidx = lax.broadcasted_iota(jnp.int32, (N_EXPERTS, tm), 0)
    work = logits
    vals, ids, hots = [], [], []
    for _ in range(TOP_K):
        m = jnp.max(work, axis=0, keepdims=True)
        sel = jnp.min(jnp.where(work == m, eidx, N_EXPERTS), axis=0, keepdims=True)
        hot = eidx == sel
        work = jnp.where(hot, -jnp.inf, work)
        vals.append(m)
        ids.append(sel)
        hots.append(hot)
    exps = [jnp.exp(v - vals[0]) for v in vals]
    denom = exps[0] + exps[1] + exps[2] + exps[3]

    @pl.when(i == 0)
    def _():
        count_acc[...] = jnp.zeros_like(count_acc)

    chosen = hots[0] | hots[1] | hots[2] | hots[3]
    earlier = (t_main < s_main).astype(BF16)
    before = _dot(chosen.astype(BF16), earlier) + count_acc[:, 0:1]
    for r in range(TOP_K):
        idx_ref[r:r + 1, :] = ids[r]
        gate_ref[r:r + 1, :] = exps[r] / denom
        rank = jnp.sum(jnp.where(hots[r], before, 0.0), axis=0, keepdims=True)
        rank_ref[r:r + 1, :] = rank.astype(jnp.int32)
    count_acc[...] += jnp.sum(chosen.astype(F32), axis=1, keepdims=True)
    count_ref[...] = count_acc[...]


def _mix_route(xf, sb, u, w_out, w_pool, pool_scale, g2, wr_split, b_router, seq_len):
    n, d = xf.shape
    tm = min(MIX_ROWS, seq_len)
    tiles_per_seq = seq_len // tm
    halo_blocks = tm // POOL_HALO
    row = lambda width: pl.BlockSpec((tm, width), lambda i: (i, 0))
    full = lambda shape: pl.BlockSpec(shape, lambda i: (0,) * len(shape))
    tok = pl.BlockSpec((TOP_K, tm), lambda i: (0, i))
    return pl.pallas_call(
        functools.partial(_mix_route_kernel, tiles_per_seq=tiles_per_seq),
        grid=(n // tm,),
        in_specs=[row(d), row(SB_WIDTH), row(POOL_WIDTH),
                  pl.BlockSpec((POOL_HALO, POOL_WIDTH), lambda i: (jnp.maximum(i * halo_blocks - 1, 0), 0)),
                  full(w_out.shape), full(w_pool.shape), full(pool_scale.shape), full(g2.shape),
                  full(wr_split.shape), full(b_router.shape)],
        out_specs=[row(d), row(d // 2), tok, tok, tok, full((N_EXPERTS, LANES))],
        out_shape=[jax.ShapeDtypeStruct((n, d), F32),
                   jax.ShapeDtypeStruct((n, d // 2), jnp.uint32),
                   jax.ShapeDtypeStruct((TOP_K, n), jnp.int32),
                   jax.ShapeDtypeStruct((TOP_K, n), F32),
                   jax.ShapeDtypeStruct((TOP_K, n), jnp.int32),
                   jax.ShapeDtypeStruct((N_EXPERTS, LANES), F32)],
        scratch_shapes=[pltpu.VMEM((N_EXPERTS, LANES), F32)],
        compiler_params=pltpu.CompilerParams(
            dimension_semantics=("arbitrary",), vmem_limit_bytes=VMEM_LIMIT_BYTES),
        name="mix_route",
    )(xf, sb, u, u, w_out, w_pool, pool_scale, g2, wr_split, b_router)


def _for_each_row_dma(tm, make_copy, action):
    def group(g, c):
        for j in range(ROW_DMA_GROUP):
            for r in range(TOP_K):
                action(make_copy(g, j, r, (g * ROW_DMA_GROUP + j) * TOP_K + r), r)
        return c

    lax.fori_loop(0, tm // ROW_DMA_GROUP, group, 0)


def _start_row_dma(copy, r):
    copy.start(priority=r % 2)


def _wait_row_dma(copy, r):
    copy.wait()


def _scatter_rows_kernel(pad_lo_ref, pad_hi_ref, dest_ref, h_ref, buf_ref, zero_ref, sem, zero_sem):
    tm = h_ref.shape[0] * ROW_DMA_GROUP

    def copy(g, j, r, flat):
        return pltpu.make_async_copy(h_ref.at[g, pl.ds(j, 1), :],
                                     buf_ref.at[pl.ds(dest_ref[flat], 1), :], sem)

    _for_each_row_dma(tm, copy, _start_row_dma)
    _for_each_row_dma(tm, copy, _wait_row_dma)

    @pl.when(pl.program_id(0) == pl.num_programs(0) - 1)
    def _():
        zero_ref[...] = jnp.zeros_like(zero_ref)
        tile = ROW_DMA_GROUP
        shift = tile.bit_length() - 1
        block = zero_ref.shape[0]

        def row_copy(p):
            return pltpu.make_async_copy(zero_ref.at[pl.ds(0, 1), :], buf_ref.at[pl.ds(p, 1), :],
                                         zero_sem)

        def tile_copy(g):
            return pltpu.make_async_copy(
                zero_ref.at[pl.ds(0, tile), :],
                buf_ref.at[pl.ds(pl.multiple_of(g * tile, tile), tile), :], zero_sem)

        def block_copy(b):
            return pltpu.make_async_copy(
                zero_ref, buf_ref.at[pl.ds(pl.multiple_of(b * block, block), block), :], zero_sem)

        def each(lo, hi, make, action):
            lax.fori_loop(lo, hi, lambda k, c: (action(make(k)), c)[1], 0)

        def expert(e, c):
            lo = pad_lo_ref[e]
            hi = pad_hi_ref[e]
            first_tile = lax.shift_right_logical(lo + (tile - 1), shift)
            head_end = jnp.minimum(lax.shift_left(first_tile, shift), hi)
            last_tile = lax.shift_right_logical(hi, shift)
            for action in (lambda cp: cp.start(), lambda cp: cp.wait()):
                each(lo, head_end, row_copy, action)
                each(first_tile, last_tile, tile_copy, action)
            return c

        lax.fori_loop(0, N_EXPERTS, expert, 0)
        first_free = pad_hi_ref[N_EXPERTS - 1] // block
        for action in (lambda cp: cp.start(), lambda cp: cp.wait()):
            each(first_free, buf_ref.shape[0] // block, block_copy, action)


def _scatter_rows(dest_flat, pad_lo, pad_hi, h2p, n_rows):
    n, width = h2p.shape
    tm = min(SCATTER_ROWS, n)
    grid_spec = pltpu.PrefetchScalarGridSpec(
        num_scalar_prefetch=2,
        grid=(n // tm,),
        in_specs=[pl.BlockSpec((tm * TOP_K,), lambda i, *_: (i,), memory_space=pltpu.SMEM),
                  pl.BlockSpec((tm // ROW_DMA_GROUP, ROW_DMA_GROUP, width), lambda i, *_: (i, 0, 0))],
        out_specs=pl.BlockSpec(memory_space=pl.ANY),
        scratch_shapes=[pltpu.VMEM((FFN_ROWS, width), h2p.dtype),
                        pltpu.SemaphoreType.DMA(()), pltpu.SemaphoreType.DMA(())],
    )
    return pl.pallas_call(
        _scatter_rows_kernel,
        grid_spec=grid_spec,
        out_shape=jax.ShapeDtypeStruct((n_rows, width), h2p.dtype),
        compiler_params=pltpu.CompilerParams(
            dimension_semantics=("arbitrary",), vmem_limit_bytes=VMEM_LIMIT_BYTES),
        name="scatter_rows",
    )(pad_lo, pad_hi, dest_flat, h2p.reshape(n // ROW_DMA_GROUP, ROW_DMA_GROUP, width))


def _expert_ffn_kernel(block_e_ref, n_used_ref, first_ref, slot_ref, next_e_ref,
                       x_ref, wgu_hbm, bgu_ref, wd_hbm, bd_ref, y_ref,
                       wgu_f32, wd_f32, wgu_bf, wd_bf, sem):
    i = pl.program_id(0)
    d_ff = wd_bf.shape[0]
    half = x_ref.shape[1]

    def fetch(e, slot):
        return (pltpu.make_async_copy(wgu_hbm.at[e], wgu_f32.at[slot], sem.at[0, slot]),
                pltpu.make_async_copy(wd_hbm.at[e], wd_f32.at[slot], sem.at[1, slot]))

    @pl.when(i < n_used_ref[0])
    def _():
        e = block_e_ref[i]
        slot = slot_ref[i]

        @pl.when(i == 0)
        def _():
            for copy in fetch(e, slot):
                copy.start()

        @pl.when(first_ref[i] == 1)
        def _():
            for copy in fetch(e, slot):
                copy.wait()
            wgu_bf[...] = wgu_f32[slot].astype(BF16)
            wd_bf[...] = wd_f32[slot].astype(BF16)

            @pl.when(next_e_ref[i] >= 0)
            def _():
                for copy in fetch(next_e_ref[i], 1 - slot):
                    copy.start()

        chunk_rows = x_ref.shape[0] // FFN_CHUNKS
        chunks = [pl.ds(c * chunk_rows, chunk_rows) for c in range(FFN_CHUNKS)]
        gus = []
        for rows in chunks:
            x_lo, x_hi = _unpack_bf16_pair(x_ref[rows, :])
            gus.append(_dot(x_lo.astype(BF16), wgu_bf[pl.ds(0, half), :])
                       + _dot(x_hi.astype(BF16), wgu_bf[pl.ds(half, half), :]) + bgu_ref[...])
        for rows, gu in zip(chunks, gus):
            gate = jnp.minimum(gu[:, :d_ff], SWIGLU_LIMIT)
            up = jnp.clip(gu[:, d_ff:], -SWIGLU_LIMIT, SWIGLU_LIMIT)
            act = (up + 1.0) * gate * (1.0 / (1.0 + jnp.exp(-SWIGLU_ALPHA * gate)))
            y = _dot(act.astype(BF16), wd_bf[...]) + bd_ref[...]
            y_ref[rows, :] = _pack_bf16_pair(y[:, :half], y[:, half:])

    @pl.when(i >= n_used_ref[0])
    def _():
        y_ref[...] = jnp.zeros_like(y_ref)


def _expert_ffn(block_e, n_used, buf, w_gate_up, b_gate_up, w_down, b_down):
    n_rows, half = buf.shape
    _, d, two_f = w_gate_up.shape
    d_ff = w_down.shape[1]
    n_blocks = n_rows // FFN_ROWS

    index = jnp.arange(n_blocks, dtype=jnp.int32)
    used = index < n_used[0]
    first = used & ((index == 0) | (block_e != jnp.roll(block_e, 1)))
    slot = (jnp.cumsum(first.astype(jnp.int32)) - 1) % 2
    first_at = jnp.where(first, index, n_blocks)
    next_first = lax.cummin(jnp.concatenate([first_at[1:], jnp.full((1,), n_blocks, jnp.int32)]),
                            reverse=True)
    next_e = jnp.where(next_first < n_blocks, block_e[jnp.minimum(next_first, n_blocks - 1)], -1)

    def rows(i, be, nu, *_):
        return (jnp.minimum(i, nu[0] - 1), 0)

    def expert(i, be, nu, *_):
        return (be[jnp.minimum(i, nu[0] - 1)], 0, 0)

    grid_spec = pltpu.PrefetchScalarGridSpec(
        num_scalar_prefetch=5,
        grid=(n_blocks,),
        in_specs=[pl.BlockSpec((FFN_ROWS, half), rows),
                  pl.BlockSpec(memory_space=pl.ANY),
                  pl.BlockSpec((None, 1, two_f), expert),
                  pl.BlockSpec(memory_space=pl.ANY),
                  pl.BlockSpec((None, 1, d), expert)],
        out_specs=pl.BlockSpec((FFN_ROWS, half), lambda i, *_: (i, 0)),
        scratch_shapes=[pltpu.VMEM((2, d, two_f), F32), pltpu.VMEM((2, d_ff, d), F32),
                        pltpu.VMEM((d, two_f), BF16), pltpu.VMEM((d_ff, d), BF16),
                        pltpu.SemaphoreType.DMA((2, 2))],
    )
    return pl.pallas_call(
        _expert_ffn_kernel,
        grid_spec=grid_spec,
        out_shape=jax.ShapeDtypeStruct((n_rows, half), jnp.uint32),
        compiler_params=pltpu.CompilerParams(
            dimension_semantics=("arbitrary",), vmem_limit_bytes=VMEM_LIMIT_BYTES),
        name="expert_ffn",
    )(block_e, n_used, first.astype(jnp.int32), slot.astype(jnp.int32), next_e.astype(jnp.int32),
      buf, w_gate_up, b_gate_up, w_down, b_down)


def _combine_kernel(dest_ref, dest_next_ref, x1_ref, gate_ref, g_ref, y_ref, o_ref, rows_ref, sem):
    tm = rows_ref.shape[2] * ROW_DMA_GROUP
    half = rows_ref.shape[4]
    step = pl.program_id(0)

    def gather(table_ref, table_offset, slot):
        def copy(g, j, r, flat):
            return pltpu.make_async_copy(y_ref.at[pl.ds(table_ref[table_offset + flat], 1), :],
                                         rows_ref.at[slot, r, g, pl.ds(j, 1), :], sem.at[slot])
        return copy

    def finish(slot):
        _for_each_row_dma(tm, gather(dest_ref, slot * tm * TOP_K, slot), _wait_row_dma)
        rows = pl.ds(slot * tm, tm)
        gates = gate_ref[rows, :]
        lo_sum = x1_ref[rows, pl.ds(0, half)]
        hi_sum = x1_ref[rows, pl.ds(half, half)]
        for r in range(TOP_K):
            lo, hi = _unpack_bf16_pair(rows_ref[slot, r].reshape(tm, half))
            g = gates[:, r:r + 1]
            lo_sum += g * lo
            hi_sum += g * hi
        ms = (jnp.sum(lo_sum * lo_sum, axis=-1, keepdims=True)
              + jnp.sum(hi_sum * hi_sum, axis=-1, keepdims=True)) * (1.0 / (2 * half))
        inv = lax.rsqrt(ms + NORM_EPS)
        o_ref[rows, pl.ds(0, half)] = lo_sum * inv * g_ref[:, pl.ds(0, half)]
        o_ref[rows, pl.ds(half, half)] = hi_sum * inv * g_ref[:, pl.ds(half, half)]

    @pl.when(step == 0)
    def _():
        _for_each_row_dma(tm, gather(dest_ref, 0, 0), _start_row_dma)

    _for_each_row_dma(tm, gather(dest_ref, tm * TOP_K, 1), _start_row_dma)
    finish(0)

    @pl.when(step + 1 < pl.num_programs(0))
    def _():
        _for_each_row_dma(tm, gather(dest_next_ref, 0, 0), _start_row_dma)

    finish(1)


def _combine(dest_flat, x1, gates_t, final_g, y):
    n, d = x1.shape
    tm = min(COMBINE_ROWS, n // 2)
    steps = n // (2 * tm)
    table = lambda index_map: pl.BlockSpec((2 * tm * TOP_K,), index_map, memory_space=pltpu.SMEM)
    return pl.pallas_call(
        _combine_kernel,
        grid=(steps,),
        in_specs=[table(lambda i: (i,)),
                  table(lambda i: (jnp.minimum(i + 1, steps - 1),)),
                  pl.BlockSpec((2 * tm, d), lambda i: (i, 0)),
                  pl.BlockSpec((2 * tm, TOP_K), lambda i: (i, 0)),
                  pl.BlockSpec((1, d), lambda i: (0, 0)),
                  pl.BlockSpec(memory_space=pl.ANY)],
        out_specs=pl.BlockSpec((2 * tm, d), lambda i: (i, 0)),
        out_shape=jax.ShapeDtypeStruct((n, d), F32),
        scratch_shapes=[pltpu.VMEM((2, TOP_K, tm // ROW_DMA_GROUP, ROW_DMA_GROUP, d // 2), jnp.uint32),
                        pltpu.SemaphoreType.DMA((2,))],
        compiler_params=pltpu.CompilerParams(
            dimension_semantics=("arbitrary",), vmem_limit_bytes=VMEM_LIMIT_BYTES),
        name="combine",
    )(dest_flat, dest_flat, x1, gates_t, final_g, y)


def _layer(x, norm1_g, w_in, sb_norm_g, w_pool, pool_scale, w_out, norm2_g,
           w_router, b_router, w_gate_up, b_gate_up, w_down, b_down, out_g):
    b, s, d = x.shape
    n = b * s
    xf = x.reshape(n, d)

    q, k, v, u = _in_proj(xf, norm1_g.reshape(1, d), w_in.astype(BF16))
    sb = _attention(q.reshape(b, s, SB_WIDTH), k.reshape(b, s, SB_WIDTH), v.reshape(b, s, SB_WIDTH),
                    sb_norm_g.reshape(1, SB_WIDTH))

    wr_t = w_router.T
    wr_hi = wr_t.astype(BF16)
    wr_lo = (wr_t - wr_hi.astype(F32)).astype(BF16)
    x1, h2p, idx, gates, rank, counts = _mix_route(
        xf, sb.reshape(n, SB_WIDTH), u, w_out.astype(BF16), w_pool.astype(BF16),
        pool_scale.reshape(1, POOL_WIDTH), norm2_g.reshape(1, d),
        jnp.stack([wr_hi, wr_lo]), b_router.reshape(N_EXPERTS, 1), s)

    counts = counts[:, 0].astype(jnp.int32)
    padded = (counts + FFN_ROWS - 1) // FFN_ROWS * FFN_ROWS
    pend = jnp.cumsum(padded)
    pstart = pend - padded
    experts = jnp.arange(N_EXPERTS, dtype=jnp.int32)[:, None, None]
    dest = rank + jnp.sum(jnp.where(idx[None] == experts, pstart[:, None, None], 0), axis=0)
    n_blocks = -(-(n * TOP_K) // FFN_ROWS) + N_EXPERTS
    n_used = (pend[-1] // FFN_ROWS).astype(jnp.int32).reshape(1)
    block_start = jnp.arange(n_blocks, dtype=jnp.int32) * FFN_ROWS
    block_e = jnp.minimum(jnp.sum(pend[None, :] <= block_start[:, None], axis=1),
                          N_EXPERTS - 1).astype(jnp.int32)

    dest_flat = dest.T.reshape(-1)
    buf = _scatter_rows(dest_flat, pstart + counts, pend, h2p, n_blocks * FFN_ROWS)
    y = _expert_ffn(block_e, n_used, buf, w_gate_up, b_gate_up.reshape(N_EXPERTS, 1, -1),
                    w_down, b_down.reshape(N_EXPERTS, 1, -1))
    out = _combine(dest_flat, x1, gates.T, out_g.reshape(1, d), y)
    return out.reshape(b, s, d)


def kernel(x, norm1_g, w_in, sb_norm_g, w_pool, pool_scale, w_out, norm2_g, w_router, b_router,
           w_gate_up, b_gate_up, w_down, b_down, final_norm_g):
    assert norm1_g.shape[0] == 1, "single-layer block"
    return _layer(x, norm1_g[0], w_in[0], sb_norm_g[0], w_pool[0], pool_scale[0], w_out[0],
                  norm2_g[0], w_router[0], b_router[0], w_gate_up[0], b_gate_up[0], w_down[0],
                  b_down[0], final_norm_g)
```

```python
import functools
import math

import jax
import jax.numpy as jnp
from jax import lax
from jax.experimental import pallas as pl
from jax.experimental.pallas import tpu as pltpu

NORM_EPS = 1e-5
SB_HEADS = 8
SB_HEAD_DIM = 64
SB_WIDTH = SB_HEADS * SB_HEAD_DIM
POOL_WINDOWS = (2, 4, 8, 16)
POOL_GROUP_DIM = 128
POOL_WIDTH = len(POOL_WINDOWS) * POOL_GROUP_DIM
POOL_HALO = 16
N_EXPERTS = 32
TOP_K = 4
SWIGLU_LIMIT = 7.0
SWIGLU_ALPHA = 1.702
EXP_UNDERFLOW_F32 = -105.0

LANES = 128
VMEM_LIMIT_BYTES = 56 * 1024 * 1024

PROJ_ROWS = 1024
ATTN_BLOCK = 256
ATTN_CHUNK = 128
ATTN_PAIRS = 2
MIX_ROWS = 512
POOL_BAND = 256
ROW_DMA_GROUP = 8
SCATTER_ROWS = 256
FFN_ROWS = 512
FFN_CHUNKS = 2
COMBINE_ROWS = 256

F32 = jnp.float32
BF16 = jnp.bfloat16


def _dot(a, b):
    return jnp.dot(a, b, preferred_element_type=F32)


def _dot_nt(a, b):
    return lax.dot_general(a, b, (((1,), (1,)), ((), ())), preferred_element_type=F32)


def _split_bf16(x):
    hi = x.astype(BF16)
    lo = (x - hi.astype(F32)).astype(BF16)
    return hi, lo


def _pack_bf16_pair(lo_f32, hi_f32):
    lo_bits = pltpu.bitcast(lo_f32.astype(BF16).astype(F32), jnp.uint32)
    hi_bits = pltpu.bitcast(hi_f32.astype(BF16).astype(F32), jnp.uint32)
    return (hi_bits & jnp.uint32(0xFFFF0000)) | (lo_bits >> 16)


def _unpack_bf16_pair(word):
    lo = pltpu.bitcast(word << 16, F32)
    hi = pltpu.bitcast(word & jnp.uint32(0xFFFF0000), F32)
    return lo, hi


def _in_proj_kernel(x_ref, g_ref, w_ref, q_ref, k_ref, v_ref, u_ref):
    x = x_ref[...]
    ms = jnp.mean(x * x, axis=-1, keepdims=True)
    h = (x * lax.rsqrt(ms + NORM_EPS) * g_ref[...]).astype(BF16)
    proj = _dot(h, w_ref[...])
    scale = 1.0 / math.sqrt(SB_HEAD_DIM)
    q_ref[...] = (proj[:, :SB_WIDTH] * scale).astype(BF16)
    k_ref[...] = proj[:, SB_WIDTH:2 * SB_WIDTH].astype(BF16)
    v_ref[...] = proj[:, 2 * SB_WIDTH:3 * SB_WIDTH].astype(BF16)
    u_ref[...] = proj[:, 3 * SB_WIDTH:].astype(BF16)


def _in_proj(xf, g, w_bf16):
    n, d = xf.shape
    tm = min(PROJ_ROWS, n)
    wide = w_bf16.shape[1]
    out = jax.ShapeDtypeStruct((n, SB_WIDTH), BF16)
    row_spec = pl.BlockSpec((tm, SB_WIDTH), lambda i: (i, 0))
    return pl.pallas_call(
        _in_proj_kernel,
        grid=(n // tm,),
        in_specs=[pl.BlockSpec((tm, d), lambda i: (i, 0)),
                  pl.BlockSpec((1, d), lambda i: (0, 0)),
                  pl.BlockSpec((d, wide), lambda i: (0, 0))],
        out_specs=[row_spec, row_spec, row_spec, row_spec],
        out_shape=[out, out, out, jax.ShapeDtypeStruct((n, POOL_WIDTH), BF16)],
        compiler_params=pltpu.CompilerParams(
            dimension_semantics=("parallel",), vmem_limit_bytes=VMEM_LIMIT_BYTES),
        name="in_proj",
    )(xf, g, w_bf16)


def _attention_kernel(q_ref, k_ref, v_ref, g_ref, o_ref, acc_ref, carry_ref):
    tq = q_ref.shape[0]
    tk = tq
    n_pairs = q_ref.shape[1] // LANES
    rows = 2 * n_pairs * tq
    n_chunks = rows // ATTN_CHUNK
    qi = pl.program_id(2)
    lane = lax.broadcasted_iota(jnp.int32, (tq, LANES), 1)
    stacked = []
    for p in range(n_pairs):
        q = q_ref[:, pl.ds(p * LANES, LANES)].astype(F32)
        stacked += [jnp.where(lane < SB_HEAD_DIM, q, 0.0), jnp.where(lane >= SB_HEAD_DIM, q, 0.0)]
    q_stack = jnp.concatenate(stacked, axis=0).astype(BF16)

    key_row = lax.broadcasted_iota(jnp.int32, (tk, tk), 0)
    key_col = lax.broadcasted_iota(jnp.int32, (tk, tk), 1)
    later = (key_row > key_col).astype(BF16)

    chunks = [slice(c * ATTN_CHUNK, (c + 1) * ATTN_CHUNK) for c in range(n_chunks)]

    def past_mask(c, keys):
        query = (c.start % tq) + lax.broadcasted_iota(jnp.int32, (ATTN_CHUNK, keys), 0)
        return lax.broadcasted_iota(jnp.int32, (ATTN_CHUNK, keys), 1) < query

    def visit(offsets, carry, first_is_diagonal):
        blocks = [[(k_ref[pl.ds(off, tk), pl.ds(p * LANES, LANES)],
                    v_ref[pl.ds(off, tk), pl.ds(p * LANES, LANES)]) for p in range(n_pairs)]
                  for off in offsets]
        pair = lambda c: c.start // (2 * tq)
        chains = [(b, c) for b in range(len(blocks)) for c in chunks]
        masked = lambda b: first_is_diagonal and b == 0
        keys = [min(tk, c.start % tq + ATTN_CHUNK) if masked(b) else tk for b, c in chains]
        z = [_dot_nt(q_stack[c], blocks[b][pair(c)][0][:n]) for (b, c), n in zip(chains, keys)]
        log_keep, split = [], []
        for (b, c), zc, n in zip(chains, z, keys):
            lk = -(jnp.maximum(zc, 0.0) + jnp.log(1.0 + jnp.exp(-jnp.abs(zc))))
            if masked(b):
                lk = jnp.where(past_mask(c, n), lk, 0.0)
            log_keep.append(lk)
            split.append(jnp.concatenate(_split_bf16(lk), axis=0))
        sums = [_dot(sp, later[:n, :n]) for sp, n in zip(split, keys)]
        carries = {c.start: (None if carry is None else carry[c]) for c in chunks}
        for i, (b, c) in enumerate(chains):
            after = sums[i][:ATTN_CHUNK] + sums[i][ATTN_CHUNK:]
            if carries[c.start] is not None:
                after = after + carries[c.start]
            a = jnp.exp(z[i] + log_keep[i] + after)
            if masked(b):
                a = jnp.where(past_mask(c, keys[i]), a, 0.0)
            values = _dot(a.astype(BF16), blocks[b][pair(c)][1][:keys[i]])
            if carry is None and b == 0:
                acc_ref[c, :] = values
            else:
                acc_ref[c, :] += values
            total = jnp.sum(log_keep[i], axis=-1, keepdims=True)
            carries[c.start] = total if carries[c.start] is None else carries[c.start] + total
        return jnp.concatenate([carries[c.start] for c in chunks], axis=0)

    start = pl.multiple_of(qi * tk, tk)

    @pl.when(qi == 0)
    def _():
        carry_ref[...] = visit([start], None, first_is_diagonal=True)

    @pl.when(qi > 0)
    def _():
        carry_ref[...] = visit([start, pl.multiple_of(start - tk, tk)], None, first_is_diagonal=True)

    def more(state):
        step, carry = state
        return (step < qi) & (jnp.max(carry) >= EXP_UNDERFLOW_F32)

    def body(state):
        step, carry = state
        off = pl.multiple_of((qi - 1 - step) * tk, tk)
        return step + 1, visit([off], carry, first_is_diagonal=False)

    lax.while_loop(more, body, (jnp.minimum(qi, 1), carry_ref[...]))

    for p in range(n_pairs):
        o = jnp.where(lane < SB_HEAD_DIM, acc_ref[pl.ds(2 * p * tq, tq), :],
                      acc_ref[pl.ds((2 * p + 1) * tq, tq), :])
        sq = o * o
        s0 = jnp.sum(jnp.where(lane < SB_HEAD_DIM, sq, 0.0), axis=-1, keepdims=True)
        s1 = jnp.sum(jnp.where(lane >= SB_HEAD_DIM, sq, 0.0), axis=-1, keepdims=True)
        ms = jnp.where(lane < SB_HEAD_DIM, s0, s1) * (1.0 / SB_HEAD_DIM)
        cols = pl.ds(p * LANES, LANES)
        o_ref[:, cols] = (o * lax.rsqrt(ms + NORM_EPS) * g_ref[:, cols]).astype(o_ref.dtype)


def _attention(q, k, v, sb_g):
    b, s, _ = q.shape
    t = min(ATTN_BLOCK, s)
    width = LANES * ATTN_PAIRS
    blk = pl.BlockSpec((None, t, width), lambda bi, p, i: (bi, i, p))
    seq = pl.BlockSpec((None, s, width), lambda bi, p, i: (bi, 0, p))
    return pl.pallas_call(
        _attention_kernel,
        grid=(b, SB_WIDTH // width, s // t),
        in_specs=[blk, seq, seq, pl.BlockSpec((1, width), lambda bi, p, i: (0, p))],
        out_specs=blk,
        out_shape=jax.ShapeDtypeStruct((b, s, SB_WIDTH), BF16),
        scratch_shapes=[pltpu.VMEM((2 * ATTN_PAIRS * t, LANES), F32),
                        pltpu.VMEM((2 * ATTN_PAIRS * t, 1), F32)],
        compiler_params=pltpu.CompilerParams(
            dimension_semantics=("parallel", "parallel", "arbitrary"),
            vmem_limit_bytes=VMEM_LIMIT_BYTES),
        name="attention",
    )(q, k, v, sb_g)


def _mix_route_kernel(x_ref, sb_ref, u_ref, halo_ref, wout_ref, wpool_ref, pscale_ref, g2_ref,
                      wr_ref, br_ref,
                      x1_ref, h2p_ref, idx_ref, gate_ref, rank_ref, count_ref,
                      count_acc, *, tiles_per_seq):
    tm = x_ref.shape[0]
    i = pl.program_id(0)
    seq_tile = i % tiles_per_seq

    band = min(POOL_BAND, tm)
    t_band = lax.broadcasted_iota(jnp.int32, (band, band), 0)
    s_band = lax.broadcasted_iota(jnp.int32, (band, band), 1)
    t_halo = lax.broadcasted_iota(jnp.int32, (band, POOL_HALO), 0)
    s_halo = lax.broadcasted_iota(jnp.int32, (band, POOL_HALO), 1) - POOL_HALO
    pos = seq_tile * tm + lax.broadcasted_iota(jnp.int32, (tm, 1), 0)
    u = u_ref[...]
    halo = jnp.where(seq_tile > 0, halo_ref[...], jnp.zeros_like(halo_ref))
    pooled_groups = []
    for g, w in enumerate(POOL_WINDOWS):
        cols = pl.ds(g * POOL_GROUP_DIM, POOL_GROUP_DIM)
        band_main = ((s_band <= t_band) & (s_band > t_band - w)).astype(BF16)
        band_halo = (s_halo > t_halo - w).astype(BF16)
        p = u[:, g * POOL_GROUP_DIM:(g + 1) * POOL_GROUP_DIM]
        wins = []
        for start in range(0, tm, band):
            before = (halo[:, g * POOL_GROUP_DIM:(g + 1) * POOL_GROUP_DIM] if start == 0
                      else p[start - POOL_HALO:start])
            wins.append(_dot(band_main, p[start:start + band]) + _dot(band_halo, before))
        win = jnp.concatenate(wins, axis=0)
        cnt = jnp.minimum(pos + 1, w).astype(F32)
        mix = win / cnt - p.astype(F32)
        pooled = _dot(mix.astype(BF16), wpool_ref[g]) * pscale_ref[:, cols]
        pooled_groups.append(pooled.astype(BF16))
    mixed = jnp.concatenate([sb_ref[...]] + pooled_groups, axis=1)
    y = _dot(mixed, wout_ref[...])

    x1 = x_ref[...] + y
    x1_ref[...] = x1
    ms = jnp.mean(x1 * x1, axis=-1, keepdims=True)
    h2 = x1 * lax.rsqrt(ms + NORM_EPS) * g2_ref[...]
    half = h2.shape[1] // 2
    h2p_ref[...] = _pack_bf16_pair(h2[:, :half], h2[:, half:])

    h_hi, h_lo = _split_bf16(h2)
    w_hi = wr_ref[0]
    w_lo = wr_ref[1]
    logits = _dot_nt(w_hi, h_hi) + _dot_nt(w_hi, h_lo) + _dot_nt(w_lo, h_hi) + br_ref[...]

    eidx = lax.broadcasted_iota(jnp.int32, (N_EXPERTS, tm), 0)
    work = logits
    vals, ids, hots = [], [], []
    for _ in range(TOP_K):
        m = jnp.max(work, axis=0, keepdims=True)
        sel = jnp.min(jnp.where(work == m, eidx, N_EXPERTS), axis=0, keepdims=True)
        hot = eidx == sel
        work = jnp.where(hot, -jnp.inf, work)
        vals.append(m)
        ids.append(sel)
        hots.append(hot)
    exps = [jnp.exp(v - vals[0]) for v in vals]
    denom = exps[0] + exps[1] + exps[2] + exps[3]

    @pl.when(i == 0)
    def _():
        count_acc[...] = jnp.zeros_like(count_acc)

    chosen = hots[0] | hots[1] | hots[2] | hots[3]
    tok_row = lax.broadcasted_iota(jnp.int32, (tm, tm), 0)
    tok_col = lax.broadcasted_iota(jnp.int32, (tm, tm), 1)
    earlier = (tok_row < tok_col).astype(BF16)
    before = _dot(chosen.astype(BF16), earlier) + count_acc[:, 0:1]
    for r in range(TOP_K):
        idx_ref[r:r + 1, :] = ids[r]
        gate_ref[r:r + 1, :] = exps[r] / denom
        rank = jnp.sum(jnp.where(hots[r], before, 0.0), axis=0, keepdims=True)
        rank_ref[r:r + 1, :] = rank.astype(jnp.int32)
    count_acc[...] += jnp.sum(chosen.astype(F32), axis=1, keepdims=True)
    count_ref[...] = count_acc[...]


def _mix_route(xf, sb, u, w_out, w_pool, pool_scale, g2, wr_split, b_router, seq_len):
    n, d = xf.shape
    tm = min(MIX_ROWS, seq_len)
    tiles_per_seq = seq_len // tm
    halo_blocks = tm // POOL_HALO
    row = lambda width: pl.BlockSpec((tm, width), lambda i: (i, 0))
    full = lambda shape: pl.BlockSpec(shape, lambda i: (0,) * len(shape))
    tok = pl.BlockSpec((TOP_K, tm), lambda i: (0, i))
    return pl.pallas_call(
        functools.partial(_mix_route_kernel, tiles_per_seq=tiles_per_seq),
        grid=(n // tm,),
        in_specs=[row(d), row(SB_WIDTH), row(POOL_WIDTH),
                  pl.BlockSpec((POOL_HALO, POOL_WIDTH), lambda i: (jnp.maximum(i * halo_blocks - 1, 0), 0)),
                  full(w_out.shape), full(w_pool.shape), full(pool_scale.shape), full(g2.shape),
                  full(wr_split.shape), full(b_router.shape)],
        out_specs=[row(d), row(d // 2), tok, tok, tok, full((N_EXPERTS, LANES))],
        out_shape=[jax.ShapeDtypeStruct((n, d), F32),
                   jax.ShapeDtypeStruct((n, d // 2), jnp.uint32),
                   jax.ShapeDtypeStruct((TOP_K, n), jnp.int32),
                   jax.ShapeDtypeStruct((TOP_K, n), F32),
                   jax.ShapeDtypeStruct((TOP_K, n), jnp.int32),
                   jax.ShapeDtypeStruct((N_EXPERTS, LANES), F32)],
        scratch_shapes=[pltpu.VMEM((N_EXPERTS, LANES), F32)],
        compiler_params=pltpu.CompilerParams(
            dimension_semantics=("arbitrary",), vmem_limit_bytes=VMEM_LIMIT_BYTES),
        name="mix_route",
    )(xf, sb, u, u, w_out, w_pool, pool_scale, g2, wr_split, b_router)


def _for_each_row_dma(tm, make_copy, action):
    def group(g, c):
        for j in range(ROW_DMA_GROUP):
            for r in range(TOP_K):
                action(make_copy(g, j, r, (g * ROW_DMA_GROUP + j) * TOP_K + r), r)
        return c

    lax.fori_loop(0, tm // ROW_DMA_GROUP, group, 0)


def _start_row_dma(copy, r):
    copy.start(priority=r % 2)


def _wait_row_dma(copy, r):
    copy.wait()


def _scatter_rows_kernel(pad_lo_ref, pad_hi_ref, dest_ref, h_ref, buf_ref, zero_ref, sem, zero_sem):
    tm = h_ref.shape[0] * ROW_DMA_GROUP

    def copy(g, j, r, flat):
        return pltpu.make_async_copy(h_ref.at[g, pl.ds(j, 1), :],
                                     buf_ref.at[pl.ds(dest_ref[flat], 1), :], sem)

    _for_each_row_dma(tm, copy, _start_row_dma)
    _for_each_row_dma(tm, copy, _wait_row_dma)

    @pl.when(pl.program_id(0) == pl.num_programs(0) - 1)
    def _():
        zero_ref[...] = jnp.zeros_like(zero_ref)
        tile = ROW_DMA_GROUP
        shift = tile.bit_length() - 1
        block = zero_ref.shape[0]

        def row_copy(p):
            return pltpu.make_async_copy(zero_ref.at[pl.ds(0, 1), :], buf_ref.at[pl.ds(p, 1), :],
                                         zero_sem)

        def tile_copy(g):
            return pltpu.make_async_copy(
                zero_ref.at[pl.ds(0, tile), :],
                buf_ref.at[pl.ds(pl.multiple_of(g * tile, tile), tile), :], zero_sem)

        def block_copy(b):
            return pltpu.make_async_copy(
                zero_ref, buf_ref.at[pl.ds(pl.multiple_of(b * block, block), block), :], zero_sem)

        def each(lo, hi, make, action):
            lax.fori_loop(lo, hi, lambda k, c: (action(make(k)), c)[1], 0)

        def expert(e, c):
            lo = pad_lo_ref[e]
            hi = pad_hi_ref[e]
            first_tile = lax.shift_right_logical(lo + (tile - 1), shift)
            head_end = jnp.minimum(lax.shift_left(first_tile, shift), hi)
            last_tile = lax.shift_right_logical(hi, shift)
            for action in (lambda cp: cp.start(), lambda cp: cp.wait()):
                each(lo, head_end, row_copy, action)
                each(first_tile, last_tile, tile_copy, action)
            return c

        lax.fori_loop(0, N_EXPERTS, expert, 0)
        first_free = pad_hi_ref[N_EXPERTS - 1] // block
        for action in (lambda cp: cp.start(), lambda cp: cp.wait()):
            each(first_free, buf_ref.shape[0] // block, block_copy, action)


def _scatter_rows(dest_flat, pad_lo, pad_hi, h2p, n_rows):
    n, width = h2p.shape
    tm = min(SCATTER_ROWS, n)
    grid_spec = pltpu.PrefetchScalarGridSpec(
        num_scalar_prefetch=2,
        grid=(n // tm,),
        in_specs=[pl.BlockSpec((tm * TOP_K,), lambda i, *_: (i,), memory_space=pltpu.SMEM),
                  pl.BlockSpec((tm // ROW_DMA_GROUP, ROW_DMA_GROUP, width), lambda i, *_: (i, 0, 0))],
        out_specs=pl.BlockSpec(memory_space=pl.ANY),
        scratch_shapes=[pltpu.VMEM((FFN_ROWS, width), h2p.dtype),
                        pltpu.SemaphoreType.DMA(()), pltpu.SemaphoreType.DMA(())],
    )
    return pl.pallas_call(
        _scatter_rows_kernel,
        grid_spec=grid_spec,
        out_shape=jax.ShapeDtypeStruct((n_rows, width), h2p.dtype),
        compiler_params=pltpu.CompilerParams(
            dimension_semantics=("arbitrary",), vmem_limit_bytes=VMEM_LIMIT_BYTES),
        name="scatter_rows",
    )(pad_lo, pad_hi, dest_flat, h2p.reshape(n // ROW_DMA_GROUP, ROW_DMA_GROUP, width))


def _expert_ffn_kernel(block_e_ref, n_used_ref, first_ref, slot_ref, next_e_ref,
                       x_ref, wgu_hbm, bgu_ref, wd_hbm, bd_ref, y_ref,
                       wgu_f32, wd_f32, wgu_bf, wd_bf, sem):
    i = pl.program_id(0)
    d_ff = wd_bf.shape[0]
    half = x_ref.shape[1]

    def fetch(e, slot):
        return (pltpu.make_async_copy(wgu_hbm.at[e], wgu_f32.at[slot], sem.at[0, slot]),
                pltpu.make_async_copy(wd_hbm.at[e], wd_f32.at[slot], sem.at[1, slot]))

    @pl.when(i < n_used_ref[0])
    def _():
        e = block_e_ref[i]
        slot = slot_ref[i]

        @pl.when(i == 0)
        def _():
            for copy in fetch(e, slot):
                copy.start()

        @pl.when(first_ref[i] == 1)
        def _():
            for copy in fetch(e, slot):
                copy.wait()
            wgu_bf[...] = wgu_f32[slot].astype(BF16)
            wd_bf[...] = wd_f32[slot].astype(BF16)

            @pl.when(next_e_ref[i] >= 0)
            def _():
                for copy in fetch(next_e_ref[i], 1 - slot):
                    copy.start()

        chunk_rows = x_ref.shape[0] // FFN_CHUNKS
        chunks = [pl.ds(c * chunk_rows, chunk_rows) for c in range(FFN_CHUNKS)]
        gus = []
        for rows in chunks:
            x_lo, x_hi = _unpack_bf16_pair(x_ref[rows, :])
            gus.append(_dot(x_lo.astype(BF16), wgu_bf[pl.ds(0, half), :])
                       + _dot(x_hi.astype(BF16), wgu_bf[pl.ds(half, half), :]) + bgu_ref[...])
        for rows, gu in zip(chunks, gus):
            gate = jnp.minimum(gu[:, :d_ff], SWIGLU_LIMIT)
            up = jnp.clip(gu[:, d_ff:], -SWIGLU_LIMIT, SWIGLU_LIMIT)
            act = (up + 1.0) * gate * (1.0 / (1.0 + jnp.exp(-SWIGLU_ALPHA * gate)))
            y = _dot(act.astype(BF16), wd_bf[...]) + bd_ref[...]
            y_ref[rows, :] = _pack_bf16_pair(y[:, :half], y[:, half:])

    @pl.when(i >= n_used_ref[0])
    def _():
        y_ref[...] = jnp.zeros_like(y_ref)


def _expert_ffn(block_e, n_used, buf, w_gate_up, b_gate_up, w_down, b_down):
    n_rows, half = buf.shape
    _, d, two_f = w_gate_up.shape
    d_ff = w_down.shape[1]
    n_blocks = n_rows // FFN_ROWS

    index = jnp.arange(n_blocks, dtype=jnp.int32)
    used = index < n_used[0]
    first = used & ((index == 0) | (block_e != jnp.roll(block_e, 1)))
    slot = (jnp.cumsum(first.astype(jnp.int32)) - 1) % 2
    first_at = jnp.where(first, index, n_blocks)
    next_first = lax.cummin(jnp.concatenate([first_at[1:], jnp.full((1,), n_blocks, jnp.int32)]),
                            reverse=True)
    next_e = jnp.where(next_first < n_blocks, block_e[jnp.minimum(next_first, n_blocks - 1)], -1)

    def rows(i, be, nu, *_):
        return (jnp.minimum(i, nu[0] - 1), 0)

    def expert(i, be, nu, *_):
        return (be[jnp.minimum(i, nu[0] - 1)], 0, 0)

    grid_spec = pltpu.PrefetchScalarGridSpec(
        num_scalar_prefetch=5,
        grid=(n_blocks,),
        in_specs=[pl.BlockSpec((FFN_ROWS, half), rows),
                  pl.BlockSpec(memory_space=pl.ANY),
                  pl.BlockSpec((None, 1, two_f), expert),
                  pl.BlockSpec(memory_space=pl.ANY),
                  pl.BlockSpec((None, 1, d), expert)],
        out_specs=pl.BlockSpec((FFN_ROWS, half), lambda i, *_: (i, 0)),
        scratch_shapes=[pltpu.VMEM((2, d, two_f), F32), pltpu.VMEM((2, d_ff, d), F32),
                        pltpu.VMEM((d, two_f), BF16), pltpu.VMEM((d_ff, d), BF16),
                        pltpu.SemaphoreType.DMA((2, 2))],
    )
    return pl.pallas_call(
        _expert_ffn_kernel,
        grid_spec=grid_spec,
        out_shape=jax.ShapeDtypeStruct((n_rows, half), jnp.uint32),
        compiler_params=pltpu.CompilerParams(
            dimension_semantics=("arbitrary",), vmem_limit_bytes=VMEM_LIMIT_BYTES),
        name="expert_ffn",
    )(block_e, n_used, first.astype(jnp.int32), slot.astype(jnp.int32), next_e.astype(jnp.int32),
      buf, w_gate_up, b_gate_up, w_down, b_down)


def _combine_kernel(dest_ref, dest_next_ref, x1_ref, gate_ref, g_ref, y_ref, o_ref, rows_ref, sem):
    tm = rows_ref.shape[2] * ROW_DMA_GROUP
    half = rows_ref.shape[4]
    step = pl.program_id(0)

    def gather(table_ref, table_offset, slot):
        def copy(g, j, r, flat):
            return pltpu.make_async_copy(y_ref.at[pl.ds(table_ref[table_offset + flat], 1), :],
                                         rows_ref.at[slot, r, g, pl.ds(j, 1), :], sem.at[slot])
        return copy

    def finish(slot):
        _for_each_row_dma(tm, gather(dest_ref, slot * tm * TOP_K, slot), _wait_row_dma)
        rows = pl.ds(slot * tm, tm)
        gates = gate_ref[rows, :]
        lo_sum = x1_ref[rows, pl.ds(0, half)]
        hi_sum = x1_ref[rows, pl.ds(half, half)]
        for r in range(TOP_K):
            lo, hi = _unpack_bf16_pair(rows_ref[slot, r].reshape(tm, half))
            g = gates[:, r:r + 1]
            lo_sum += g * lo
            hi_sum += g * hi
        ms = (jnp.sum(lo_sum * lo_sum, axis=-1, keepdims=True)
              + jnp.sum(hi_sum * hi_sum, axis=-1, keepdims=True)) * (1.0 / (2 * half))
        inv = lax.rsqrt(ms + NORM_EPS)
        o_ref[rows, pl.ds(0, half)] = lo_sum * inv * g_ref[:, pl.ds(0, half)]
        o_ref[rows, pl.ds(half, half)] = hi_sum * inv * g_ref[:, pl.ds(half, half)]

    @pl.when(step == 0)
    def _():
        _for_each_row_dma(tm, gather(dest_ref, 0, 0), _start_row_dma)

    _for_each_row_dma(tm, gather(dest_ref, tm * TOP_K, 1), _start_row_dma)
    finish(0)

    @pl.when(step + 1 < pl.num_programs(0))
    def _():
        _for_each_row_dma(tm, gather(dest_next_ref, 0, 0), _start_row_dma)

    finish(1)


def _combine(dest_flat, x1, gates_t, final_g, y):
    n, d = x1.shape
    tm = min(COMBINE_ROWS, n // 2)
    steps = n // (2 * tm)
    table = lambda index_map: pl.BlockSpec((2 * tm * TOP_K,), index_map, memory_space=pltpu.SMEM)
    return pl.pallas_call(
        _combine_kernel,
        grid=(steps,),
        in_specs=[table(lambda i: (i,)),
                  table(lambda i: (jnp.minimum(i + 1, steps - 1),)),
                  pl.BlockSpec((2 * tm, d), lambda i: (i, 0)),
                  pl.BlockSpec((2 * tm, TOP_K), lambda i: (i, 0)),
                  pl.BlockSpec((1, d), lambda i: (0, 0)),
                  pl.BlockSpec(memory_space=pl.ANY)],
        out_specs=pl.BlockSpec((2 * tm, d), lambda i: (i, 0)),
        out_shape=jax.ShapeDtypeStruct((n, d), F32),
        scratch_shapes=[pltpu.VMEM((2, TOP_K, tm // ROW_DMA_GROUP, ROW_DMA_GROUP, d // 2), jnp.uint32),
                        pltpu.SemaphoreType.DMA((2,))],
        compiler_params=pltpu.CompilerParams(
            dimension_semantics=("arbitrary",), vmem_limit_bytes=VMEM_LIMIT_BYTES),
        name="combine",
    )(dest_flat, dest_flat, x1, gates_t, final_g, y)


def _layer(x, norm1_g, w_in, sb_norm_g, w_pool, pool_scale, w_out, norm2_g,
           w_router, b_router, w_gate_up, b_gate_up, w_down, b_down, out_g):
    b, s, d = x.shape
    n = b * s
    xf = x.reshape(n, d)

    q, k, v, u = _in_proj(xf, norm1_g.reshape(1, d), w_in.astype(BF16))
    sb = _attention(q.reshape(b, s, SB_WIDTH), k.reshape(b, s, SB_WIDTH), v.reshape(b, s, SB_WIDTH),
                    sb_norm_g.reshape(1, SB_WIDTH))

    wr_t = w_router.T
    wr_hi = wr_t.astype(BF16)
    wr_lo = (wr_t - wr_hi.astype(F32)).astype(BF16)
    x1, h2p, idx, gates, rank, counts = _mix_route(
        xf, sb.reshape(n, SB_WIDTH), u, w_out.astype(BF16), w_pool.astype(BF16),
        pool_scale.reshape(1, POOL_WIDTH), norm2_g.reshape(1, d),
        jnp.stack([wr_hi, wr_lo]), b_router.reshape(N_EXPERTS, 1), s)

    counts = counts[:, 0].astype(jnp.int32)
    padded = (counts + FFN_ROWS - 1) // FFN_ROWS * FFN_ROWS
    pend = jnp.cumsum(padded)
    pstart = pend - padded
    experts = jnp.arange(N_EXPERTS, dtype=jnp.int32)[:, None, None]
    dest = rank + jnp.sum(jnp.where(idx[None] == experts, pstart[:, None, None], 0), axis=0)
    n_blocks = -(-(n * TOP_K) // FFN_ROWS) + N_EXPERTS
    n_used = (pend[-1] // FFN_ROWS).astype(jnp.int32).reshape(1)
    block_start = jnp.arange(n_blocks, dtype=jnp.int32) * FFN_ROWS
    block_e = jnp.minimum(jnp.sum(pend[None, :] <= block_start[:, None], axis=1),
                          N_EXPERTS - 1).astype(jnp.int32)

    dest_flat = dest.T.reshape(-1)
    buf = _scatter_rows(dest_flat, pstart + counts, pend, h2p, n_blocks * FFN_ROWS)
    y = _expert_ffn(block_e, n_used, buf, w_gate_up, b_gate_up.reshape(N_EXPERTS, 1, -1),
                    w_down, b_down.reshape(N_EXPERTS, 1, -1))
    out = _combine(dest_flat, x1, gates.T, out_g.reshape(1, d), y)
    return out.reshape(b, s, d)


def kernel(x, norm1_g, w_in, sb_norm_g, w_pool, pool_scale, w_out, norm2_g, w_router, b_router,
           w_gate_up, b_gate_up, w_down, b_down, final_norm_g):
    assert norm1_g.shape[0] == 1, "single-layer block"
    return _layer(x, norm1_g[0], w_in[0], sb_norm_g[0], w_pool[0], pool_scale[0], w_out[0],
                  norm2_g[0], w_router[0], b_router[0], w_gate_up[0], b_gate_up[0], w_down[0],
                  b_down[0], final_norm_g)
```
